```python
import math
import jax
import jax.numpy as jnp
from jax import lax
import numpy as np

D_MODEL = 1024
BATCH = 1
SEQ = 16384
DEPTH = 2
DEC_BATCH = 32
DEC_SEQ = 4
PAST_LEN = 16384
PAGE_SIZE = 128

N_MAMBA_LAYERS = (DEPTH + 1) // 2
N_ATT_LAYERS = DEPTH // 2
SSD_HEAD_DIM = 64
SSD_WIDTH = D_MODEL
SSD_HEADS = SSD_WIDTH // SSD_HEAD_DIM
SSD_GROUPS = 4
SSD_STATE = 128
SSD_CONV = 4
SSD_CHUNK = 256
SSD_CONV_DIM = SSD_WIDTH + 2 * SSD_GROUPS * SSD_STATE
CONF_WIDTH = D_MODEL
CONF_CONV_WIDTH = 31
IN0_COLS = SSD_WIDTH + SSD_CONV_DIM + SSD_HEADS + 2 * CONF_WIDTH + CONF_WIDTH
MIX0_WIDTH = SSD_WIDTH + CONF_WIDTH
ATT_HEADS = 16
ATT_KV_HEADS = 4
ATT_GROUP = ATT_HEADS // ATT_KV_HEADS
HEAD_DIM = 64
ATT_WIDTH = ATT_HEADS * HEAD_DIM
KV_WIDTH = ATT_KV_HEADS * HEAD_DIM
IN1_COLS = 2 * ATT_WIDTH + 2 * KV_WIDTH
MOBA_BLOCK = 256
MOBA_TOP_K = 3
Q_CHUNK = 64
SCALE = HEAD_DIM ** -0.5
NORM_EPS = 1e-6

kernel_name = 'ssd_conformer_moba_hybrid_step'


def rms_norm(x, g):
    xf = x.astype(jnp.float32)
    y = xf * lax.rsqrt(jnp.mean(xf * xf, axis=-1, keepdims=True) + NORM_EPS)
    return (y * g.astype(jnp.float32)).astype(x.dtype)


def layer_norm(x, g, b):
    xf = x.astype(jnp.float32)
    mu = jnp.mean(xf, axis=-1, keepdims=True)
    xc = xf - mu
    var = jnp.mean(xc * xc, axis=-1, keepdims=True)
    return (xc * lax.rsqrt(var + NORM_EPS) * g.astype(jnp.float32) + b.astype(jnp.float32)).astype(x.dtype)


def causal_depthwise_conv(x_full, w, b):
    y = lax.conv_general_dilated(x_full, w[:, None, :].astype(x_full.dtype), (1,), 'VALID',
                                 dimension_numbers=('NWC', 'WIO', 'NWC'), feature_group_count=w.shape[-1])
    return y + b.astype(y.dtype)


def ssd_chunked_scan(x, dt, a_head, bm, cm, h0, chunk):
    f32 = jnp.float32
    b, t, h, p = x.shape
    g, n = bm.shape[-2:]
    r = h // g
    nc = -(-t // chunk)
    pad = nc * chunk - t
    xdt = x.astype(f32) * dt[..., None]
    a = dt * a_head
    bf = bm.astype(f32)
    cf = cm.astype(f32)
    if pad:
        padt = lambda arr: jnp.pad(arr, [(0, 0), (0, pad)] + [(0, 0)] * (arr.ndim - 2))
        xdt, a, bf, cf = padt(xdt), padt(a), padt(bf), padt(cf)
    xc = xdt.reshape(b, nc, chunk, g, r, p)
    ac = a.reshape(b, nc, chunk, g, r)
    bc = bf.reshape(b, nc, chunk, g, n)
    cc = cf.reshape(b, nc, chunk, g, n)
    acs = jnp.cumsum(ac, axis=2)
    causal = jnp.tril(jnp.ones((chunk, chunk), bool))
    seg = acs[:, :, :, None] - acs[:, :, None, :]
    decay = jnp.exp(jnp.where(causal[None, None, :, :, None, None], seg, -jnp.inf))
    cb = jnp.einsum('bclgn,bcsgn->bclsg', cc, bc)
    y_diag = jnp.einsum('bclsgr,bcsgrp->bclgrp', decay * cb[..., None], xc)
    decay_end = jnp.exp(acs[:, :, -1:] - acs)
    chunk_states = jnp.einsum('bclgn,bclgr,bclgrp->bcgrpn', bc, decay_end, xc)
    chunk_decay = jnp.exp(acs[:, :, -1])

    def step(hc, inp):
        s_c, d_c = inp
        return hc * d_c[..., None, None] + s_c, hc

    h_t, h_in = lax.scan(step, h0.astype(f32).reshape(b, g, r, p, n),
                         (jnp.moveaxis(chunk_states, 1, 0), jnp.moveaxis(chunk_decay, 1, 0)))
    h_in = jnp.moveaxis(h_in, 0, 1)
    y_off = jnp.einsum('bclgn,bcgrpn->bclgrp', cc, h_in) * jnp.exp(acs)[..., None]
    y = (y_diag + y_off).reshape(b, nc * chunk, h, p)[:, :t]
    return y, h_t.reshape(b, h, p, n)


def hybrid_layer(x, h0, ssd_hist, conf_hist, norm_g, w_in, conv_w, conv_b, dt_bias, a_log, d_skip,
                 ssd_norm_g, cconv_w, cconv_b, cln_g, cln_b, w_out):
    f32 = jnp.float32
    b, t, _ = x.shape
    u = rms_norm(x, norm_g) @ w_in
    o1 = SSD_WIDTH
    o2 = o1 + SSD_CONV_DIM
    o3 = o2 + SSD_HEADS
    o4 = o3 + 2 * CONF_WIDTH
    z, xbc, dt_raw, glu_in, c_gate = jnp.split(u, [o1, o2, o3, o4], axis=-1)
    xbc_full = jnp.concatenate([ssd_hist.astype(u.dtype), xbc], axis=1)
    new_ssd_hist = xbc_full[:, xbc_full.shape[1] - (SSD_CONV - 1):]
    xbc = jax.nn.silu(causal_depthwise_conv(xbc_full, conv_w, conv_b))
    xs, bm, cm = jnp.split(xbc, [SSD_WIDTH, SSD_WIDTH + SSD_GROUPS * SSD_STATE], axis=-1)
    xs = xs.reshape(b, t, SSD_HEADS, SSD_HEAD_DIM)
    bm = bm.reshape(b, t, SSD_GROUPS, SSD_STATE)
    cm = cm.reshape(b, t, SSD_GROUPS, SSD_STATE)
    dt = jax.nn.softplus(dt_raw.astype(f32) + dt_bias.astype(f32))
    a_head = -jnp.exp(a_log.astype(f32))
    y, h_t = ssd_chunked_scan(xs, dt, a_head, bm, cm, h0, min(SSD_CHUNK, t))
    y = y + d_skip.astype(f32)[:, None] * xs.astype(f32)
    y = y.reshape(b, t, SSD_WIDTH) * jax.nn.silu(z.astype(f32))
    yg = y.reshape(b, t, SSD_GROUPS, SSD_WIDTH // SSD_GROUPS)
    yg = yg * lax.rsqrt(jnp.mean(yg * yg, axis=-1, keepdims=True) + NORM_EPS)
    y = (yg.reshape(b, t, SSD_WIDTH) * ssd_norm_g.astype(f32)).astype(x.dtype)
    ga, gb = jnp.split(glu_in, 2, axis=-1)
    gl = ga * jax.nn.sigmoid(gb)
    g_full = jnp.concatenate([conf_hist.astype(gl.dtype), gl], axis=1)
    new_conf_hist = g_full[:, g_full.shape[1] - (CONF_CONV_WIDTH - 1):]
    c = layer_norm(causal_depthwise_conv(g_full, cconv_w, cconv_b), cln_g, cln_b)
    c = jax.nn.silu(c) * jax.nn.silu(c_gate)
    out = jnp.concatenate([y, c.astype(x.dtype)], axis=-1) @ w_out
    return out, h_t, new_ssd_hist, new_conf_hist


def att_project(x, norm_g, w_in, qn_g, kn_g):
    b, t, _ = x.shape
    u = rms_norm(x, norm_g) @ w_in
    q, k, v, gate = jnp.split(u, [ATT_WIDTH, ATT_WIDTH + KV_WIDTH, ATT_WIDTH + 2 * KV_WIDTH], axis=-1)
    q = rms_norm(q.reshape(b, t, ATT_KV_HEADS, ATT_GROUP, HEAD_DIM), qn_g)
    k = rms_norm(k.reshape(b, t, ATT_KV_HEADS, HEAD_DIM), kn_g)
    v = v.reshape(b, t, ATT_KV_HEADS, HEAD_DIM)
    return q, k, v, gate


def att_output(o, gate, w_out):
    return (o * jax.nn.silu(gate)) @ w_out


def moba_attend(q, k_own, v_own, own_mask, k_sel, v_sel, sel_mask):
    s_own = jnp.einsum('bthgd,bjhd->bthgj', q, k_own).astype(jnp.float32) * SCALE
    s_own = jnp.where(own_mask, s_own, -jnp.inf)
    if k_sel is None:
        p = jax.nn.softmax(s_own, axis=-1).astype(v_own.dtype)
        return jnp.einsum('bthgj,bjhd->bthgd', p, v_own)
    s_sel = jnp.einsum('bthgd,bthgkjd->bthgkj', q, k_sel).astype(jnp.float32) * SCALE
    s_sel = jnp.where(sel_mask, s_sel, -jnp.inf)
    b, t, hk, g, kk, j = s_sel.shape
    s = jnp.concatenate([s_sel.reshape(b, t, hk, g, kk * j), s_own], axis=-1)
    p = jax.nn.softmax(s, axis=-1).astype(v_own.dtype)
    p_sel = p[..., :kk * j].reshape(b, t, hk, g, kk, j)
    p_own = p[..., kk * j:]
    return (jnp.einsum('bthgkj,bthgkjd->bthgd', p_sel, v_sel)
            + jnp.einsum('bthgj,bjhd->bthgd', p_own, v_own))


def moba_prompt(q, k, v):
    f32 = jnp.float32
    b, s = q.shape[:2]
    nb = -(-s // MOBA_BLOCK)
    pad = ((0, 0), (0, nb * MOBA_BLOCK - s), (0, 0), (0, 0))
    kp = jnp.pad(k, pad)
    vp = jnp.pad(v, pad)
    ksel = min(MOBA_TOP_K, nb - 1)
    nq = s // Q_CHUNK

    def chunked(arr):
        return jnp.moveaxis(arr.reshape((b, nq, Q_CHUNK) + arr.shape[2:]), 1, 0)

    xs = (jnp.arange(nq), chunked(q))
    if ksel > 0:
        kb = kp.reshape(b, nb, MOBA_BLOCK, ATT_KV_HEADS, HEAD_DIM)
        vb = vp.reshape(b, nb, MOBA_BLOCK, ATT_KV_HEADS, HEAD_DIM)
        own = jnp.arange(s) // MOBA_BLOCK
        kmean = jnp.mean(kb.astype(f32), axis=2)
        gate = jnp.einsum('bshgd,bnhd->bshgn', q.astype(f32), kmean)
        is_past = jnp.arange(nb)[None, :] < own[:, None]
        gate = jnp.where(is_past[None, :, None, None, :], gate, -jnp.inf)
        idx = lax.top_k(gate, ksel)[1]
        valid = idx < own[None, :, None, None, None]
        xs = xs + (chunked(idx), chunked(valid))
        kbh = jnp.moveaxis(kb, 3, 1)
        vbh = jnp.moveaxis(vb, 3, 1)
        bi = jnp.arange(b)[:, None, None, None, None]
        hi = jnp.arange(ATT_KV_HEADS)[None, None, :, None, None]

    def one_chunk(args):
        ci, qi = args[0], args[1]
        start = ci * Q_CHUNK
        blk_start = (start // MOBA_BLOCK) * MOBA_BLOCK
        k_own = lax.dynamic_slice_in_dim(kp, blk_start, MOBA_BLOCK, axis=1)
        v_own = lax.dynamic_slice_in_dim(vp, blk_start, MOBA_BLOCK, axis=1)
        qpos = start + jnp.arange(Q_CHUNK)
        kpos = blk_start + jnp.arange(MOBA_BLOCK)
        own_mask = (kpos[None, :] <= qpos[:, None])[None, :, None, None, :]
        if ksel == 0:
            return moba_attend(qi, k_own, v_own, own_mask, None, None, None)
        ii, vi = args[2], args[3]
        return moba_attend(qi, k_own, v_own, own_mask, kbh[bi, hi, ii], vbh[bi, hi, ii], vi[..., None])

    o = lax.map(one_chunk, xs)
    return jnp.moveaxis(o, 0, 1).reshape(b, s, ATT_WIDTH)


def moba_sample(q, k_new, v_new, k_pool, v_pool, page_table):
    f32 = jnp.float32
    db, t = q.shape[:2]
    n_pages = page_table.shape[1]
    ppb = MOBA_BLOCK // PAGE_SIZE
    ob = (n_pages * PAGE_SIZE) // MOBA_BLOCK
    n_own_pages = n_pages - ob * ppb
    own_pages = page_table[:, ob * ppb:ob * ppb + n_own_pages]
    n_c = n_own_pages * PAGE_SIZE
    k_own = jnp.concatenate([k_pool[own_pages].reshape(db, n_c, ATT_KV_HEADS, HEAD_DIM).astype(k_new.dtype), k_new], axis=1)
    v_own = jnp.concatenate([v_pool[own_pages].reshape(db, n_c, ATT_KV_HEADS, HEAD_DIM).astype(v_new.dtype), v_new], axis=1)
    own_mask = jnp.concatenate([jnp.ones((t, n_c), bool), jnp.tril(jnp.ones((t, t), bool))], axis=1)
    own_mask = own_mask[None, :, None, None, :]
    ksel = min(MOBA_TOP_K, ob)
    if ksel == 0:
        o = moba_attend(q, k_own, v_own, own_mask, None, None, None)
        return o.reshape(db, t, ATT_WIDTH)
    past_pages = page_table[:, :ob * ppb]
    kmean = jnp.mean(k_pool[past_pages].astype(f32).reshape(db, ob, MOBA_BLOCK, ATT_KV_HEADS, HEAD_DIM), axis=2)
    gate = jnp.einsum('bthgd,bnhd->bthgn', q.astype(f32), kmean)
    idx = lax.top_k(gate, ksel)[1]
    bi = jnp.arange(db)[:, None, None, None, None, None]
    phys = page_table[bi, idx[..., None] * ppb + jnp.arange(ppb)]
    hi = jnp.arange(ATT_KV_HEADS)[None, None, :, None, None, None, None]
    rows = jnp.arange(PAGE_SIZE)
    sel_shape = (db, t, ATT_KV_HEADS, ATT_GROUP, ksel, MOBA_BLOCK, HEAD_DIM)
    k_sel = k_pool[phys[..., None], rows, hi].reshape(sel_shape).astype(k_new.dtype)
    v_sel = v_pool[phys[..., None], rows, hi].reshape(sel_shape).astype(v_new.dtype)
    o = moba_attend(q, k_own, v_own, own_mask, k_sel, v_sel, True)
    return o.reshape(db, t, ATT_WIDTH)


def setup_inputs(seed: int = 0) -> dict:
    key = jax.random.key(seed)
    ks = iter(jax.random.split(key, 40))
    f32 = jnp.float32
    nrm = lambda shape, s: s * jax.random.normal(next(ks), shape, f32)
    nm, na = N_MAMBA_LAYERS, N_ATT_LAYERS
    n_pages = PAST_LEN // PAGE_SIZE
    n_used = DEC_BATCH * n_pages
    n_pool = n_used + (n_used + 3) // 4
    page_table = jax.random.permutation(next(ks), n_pool)[:n_used].reshape(DEC_BATCH, n_pages).astype(jnp.int32)
    dt0 = jnp.exp(jax.random.uniform(next(ks), (nm, SSD_HEADS), f32, math.log(1e-3), math.log(1e-1)))
    dt_bias = dt0 + jnp.log(-jnp.expm1(-dt0))
    a_log = jnp.log(jax.random.uniform(next(ks), (nm, SSD_HEADS), f32, 1.0, 16.0))
    return {
        'x_prompt': nrm((BATCH, SEQ, D_MODEL), 1.0),
        'x_sample': nrm((DEC_BATCH, DEC_SEQ, D_MODEL), 1.0),
        'state_ssm': nrm((nm, DEC_BATCH, SSD_HEADS, SSD_HEAD_DIM, SSD_STATE), 0.1),
        'state_ssd_conv': nrm((nm, DEC_BATCH, SSD_CONV - 1, SSD_CONV_DIM), 1.0),
        'state_conf_conv': nrm((nm, DEC_BATCH, CONF_CONV_WIDTH - 1, CONF_WIDTH), 0.5),
        'cache_k': nrm((na, n_pool, PAGE_SIZE, ATT_KV_HEADS, HEAD_DIM), 1.0),
        'cache_v': nrm((na, n_pool, PAGE_SIZE, ATT_KV_HEADS, HEAD_DIM), 1.0),
        'page_table': page_table,
        'norm0_g': 1.0 + nrm((nm, D_MODEL), 0.05),
        'w_in0': nrm((nm, D_MODEL, IN0_COLS), D_MODEL ** -0.5),
        'ssd_conv_w': nrm((nm, SSD_CONV, SSD_CONV_DIM), SSD_CONV ** -0.5),
        'ssd_conv_b': nrm((nm, SSD_CONV_DIM), 0.02),
        'ssd_dt_bias': dt_bias,
        'ssd_a_log': a_log,
        'ssd_d': 1.0 + nrm((nm, SSD_HEADS), 0.1),
        'ssd_norm_g': 1.0 + nrm((nm, SSD_WIDTH), 0.05),
        'conf_conv_w': nrm((nm, CONF_CONV_WIDTH, CONF_WIDTH), CONF_CONV_WIDTH ** -0.5),
        'conf_conv_b': nrm((nm, CONF_WIDTH), 0.02),
        'conf_ln_g': 1.0 + nrm((nm, CONF_WIDTH), 0.05),
        'conf_ln_b': nrm((nm, CONF_WIDTH), 0.02),
        'w_out0': nrm((nm, MIX0_WIDTH, D_MODEL), 0.5 * MIX0_WIDTH ** -0.5),
        'norm1_g': 1.0 + nrm((na, D_MODEL), 0.05),
        'w_in1': nrm((na, D_MODEL, IN1_COLS), D_MODEL ** -0.5),
        'q_norm_g': 1.0 + nrm((na, HEAD_DIM), 0.05),
        'k_norm_g': 1.0 + nrm((na, HEAD_DIM), 0.05),
        'w_out1': nrm((na, ATT_WIDTH, D_MODEL), 0.5 * ATT_WIDTH ** -0.5),
    }


def reference(x_prompt, x_sample, state_ssm, state_ssd_conv, state_conf_conv, cache_k, cache_v, page_table,
              norm0_g, w_in0, ssd_conv_w, ssd_conv_b, ssd_dt_bias, ssd_a_log, ssd_d, ssd_norm_g,
              conf_conv_w, conf_conv_b, conf_ln_g, conf_ln_b, w_out0,
              norm1_g, w_in1, q_norm_g, k_norm_g, w_out1):
    yp, ys = x_prompt, x_sample
    ssm_p, ssm_s, sconv_p, sconv_s, cconv_p, cconv_s = [], [], [], [], [], []
    k_p, v_p, k_s, v_s = [], [], [], []
    for layer in range(DEPTH):
        i = layer // 2
        if layer % 2 == 0:
            w = (norm0_g[i], w_in0[i], ssd_conv_w[i], ssd_conv_b[i], ssd_dt_bias[i], ssd_a_log[i], ssd_d[i],
                 ssd_norm_g[i], conf_conv_w[i], conf_conv_b[i], conf_ln_g[i], conf_ln_b[i], w_out0[i])
            bp = yp.shape[0]
            out_p, h_pn, sc_pn, cc_pn = hybrid_layer(
                yp, jnp.zeros((bp, SSD_HEADS, SSD_HEAD_DIM, SSD_STATE), jnp.float32),
                jnp.zeros((bp, SSD_CONV - 1, SSD_CONV_DIM), yp.dtype),
                jnp.zeros((bp, CONF_CONV_WIDTH - 1, CONF_WIDTH), yp.dtype), *w)
            out_s, h_sn, sc_sn, cc_sn = hybrid_layer(ys, state_ssm[i], state_ssd_conv[i], state_conf_conv[i], *w)
            yp = yp + out_p
            ys = ys + out_s
            ssm_p.append(h_pn)
            ssm_s.append(h_sn)
            sconv_p.append(sc_pn)
            sconv_s.append(sc_sn)
            cconv_p.append(cc_pn)
            cconv_s.append(cc_sn)
        else:
            qp, kpn, vpn, gp = att_project(yp, norm1_g[i], w_in1[i], q_norm_g[i], k_norm_g[i])
            yp = yp + att_output(moba_prompt(qp, kpn, vpn), gp, w_out1[i])
            qs, ksn, vsn, gs = att_project(ys, norm1_g[i], w_in1[i], q_norm_g[i], k_norm_g[i])
            ys = ys + att_output(moba_sample(qs, ksn, vsn, cache_k[i], cache_v[i], page_table), gs, w_out1[i])
            k_p.append(kpn)
            v_p.append(vpn)
            k_s.append(ksn)
            v_s.append(vsn)
    return (yp, ys, jnp.stack(ssm_p), jnp.stack(ssm_s), jnp.stack(sconv_p), jnp.stack(sconv_s),
            jnp.stack(cconv_p), jnp.stack(cconv_s), jnp.stack(k_p), jnp.stack(v_p), jnp.stack(k_s), jnp.stack(v_s))
```

```python
import functools

import jax
import jax.numpy as jnp
from jax import lax
from jax.experimental import pallas as pl
from jax.experimental.pallas import tpu as pltpu

F32 = jnp.float32
BF16 = jnp.bfloat16
NORM_EPS = 1e-6
MASKED = -1e30
LANES = 128
CHUNK = 256
D_MODEL = 1024
SSD_HEADS = 16
SSD_HEAD_DIM = 64
SSD_GROUPS = 4
SSD_STATE = 128
SSD_CONV = 4
SSD_CONV_DIM = 2048
CONF_CONV = 31
ATT_HEADS = 16
KV_HEADS = 4
HEAD_DIM = 64
KV_WIDTH = KV_HEADS * HEAD_DIM
MOBA_TOP_K = 3
PAGE = 128
C_Z, C_XBC, C_GA, C_GB, C_CG, C_DT, IN0_PAD = 0, 1024, 3072, 4096, 5120, 6144, 6272
SSD_HIST_PAD = 8
CONF_HIST_PAD = 32
VMEM_LIMIT = 56 * 1024 * 1024

_NT = (((1,), (1,)), ((), ()))


def _sigmoid(x):
    return 1.0 / (1.0 + jnp.exp(-x))


def _silu(x):
    return x * _sigmoid(x)


def _softplus(x):
    return jnp.maximum(x, 0.0) + jnp.log1p(jnp.exp(-jnp.abs(x)))


def _rms_rows(x, g):
    ms = jnp.mean(x * x, axis=-1, keepdims=True)
    return x * lax.rsqrt(ms + NORM_EPS) * g


def _const_spec(shape):
    nd = len(shape)
    return pl.BlockSpec(shape, lambda *_: (0,) * nd)


def _ssd_conv_silu(buf_ref, cw_ref, cb_ref, rows, out_ref):
    for c in range(0, SSD_CONV_DIM, 512):
        acc = cb_ref[:, c:c + 512]
        for k in range(SSD_CONV):
            o = SSD_HIST_PAD - (SSD_CONV - 1) + k
            acc = acc + cw_ref[k:k + 1, c:c + 512] * buf_ref[o:o + rows, c:c + 512]
        out_ref[:, c:c + 512] = _silu(acc)


def _gated_group_norm(y, z, sng_ref, yc_ref):
    y = y * _silu(z)
    gw = D_MODEL // SSD_GROUPS
    for g in range(SSD_GROUPS):
        yg = y[:, g * gw:(g + 1) * gw]
        r = lax.rsqrt(jnp.mean(yg * yg, axis=-1, keepdims=True) + NORM_EPS)
        yc_ref[:, g * gw:(g + 1) * gw] = (yg * r * sng_ref[:, g * gw:(g + 1) * gw]).astype(yc_ref.dtype)


def _conformer(gl_ref, rows, cgate, ccw_ref, ccb_ref, lng_ref, lnb_ref, conv_ref, yc_ref):
    for c in range(0, D_MODEL, 256):
        acc = ccb_ref[:, c:c + 256]
        for k in range(CONF_CONV):
            o = CONF_HIST_PAD - (CONF_CONV - 1) + k
            acc = acc + ccw_ref[k:k + 1, c:c + 256] * gl_ref[o:o + rows, c:c + 256]
        conv_ref[:, c:c + 256] = acc
    cv = conv_ref[...]
    mu = jnp.mean(cv, axis=-1, keepdims=True)
    xc = cv - mu
    var = jnp.mean(xc * xc, axis=-1, keepdims=True)
    cn = xc * lax.rsqrt(var + NORM_EPS) * lng_ref[...] + lnb_ref[...]
    yc_ref[:, D_MODEL:2 * D_MODEL] = (_silu(cn) * _silu(cgate)).astype(yc_ref.dtype)


def _l0_prompt_kernel(x_ref, ng_ref, w_ref, cw_ref, cb_ref, dtb_ref, alog_ref, dcol_ref, sng_ref,
                      ccw_ref, ccb_ref, lng_ref, lnb_ref, wo_ref, h0_ref, sh0_ref, ch0_ref,
                      y_ref, hT_ref, sh_ref, chh_ref,
                      u_s, xbc_s, gl_s, h_s, act_s, yT_s, conv_s, yc_s):
    i = pl.program_id(0)
    L = CHUNK

    @pl.when(i == 0)
    def _():
        h_s[...] = h0_ref[...]
        xbc_s[0:SSD_HIST_PAD, :] = sh0_ref[...]
        gl_s[0:CONF_HIST_PAD, :] = ch0_ref[...]

    x = x_ref[...]
    xn = _rms_rows(x, ng_ref[...]).astype(BF16)
    nblk = IN0_PAD // 896
    for c in range(nblk):
        u_s[:, c * 896:(c + 1) * 896] = jnp.dot(xn, w_ref[:, c * 896:(c + 1) * 896], preferred_element_type=F32)

    xbc_s[SSD_HIST_PAD:SSD_HIST_PAD + L, :] = u_s[:, C_XBC:C_XBC + SSD_CONV_DIM]
    _ssd_conv_silu(xbc_s, cw_ref, cb_ref, L, act_s)

    dt = _softplus(u_s[:, C_DT:C_DT + LANES] + dtb_ref[...])
    a = dt * (-jnp.exp(alog_ref[...]))
    r_i = lax.broadcasted_iota(jnp.int32, (L, L), 0)
    c_i = lax.broadcasted_iota(jnp.int32, (L, L), 1)
    tri = (c_i <= r_i).astype(F32)
    acs = jnp.dot(tri, a, precision=lax.Precision.HIGHEST, preferred_element_type=F32)
    acsT = acs.T
    dtT = dt.T
    last = acsT[:, L - 1:L]
    exp_acsT = jnp.exp(acsT)
    dec_endT = jnp.exp(last - acsT)
    chunk_dec = jnp.exp(last)
    causalT = r_i <= c_i

    xT = act_s[:, 0:D_MODEL].T
    P = SSD_HEAD_DIM
    gp = SSD_HEADS // SSD_GROUPS * P
    for g in range(SSD_GROUPS):
        b_g = act_s[:, D_MODEL + g * SSD_STATE:D_MODEL + (g + 1) * SSD_STATE].astype(BF16)
        c_g = act_s[:, D_MODEL + (SSD_GROUPS + g) * SSD_STATE:D_MODEL + (SSD_GROUPS + g + 1) * SSD_STATE].astype(BF16)
        cbT = lax.dot_general(b_g, c_g, _NT, preferred_element_type=F32)
        h_g = h_s[g * gp:(g + 1) * gp, :]
        y_offT = lax.dot_general(h_g.astype(BF16), c_g, _NT, preferred_element_type=F32)
        st_lhs = []
        for r in range(SSD_HEADS // SSD_GROUPS):
            h = g * (SSD_HEADS // SSD_GROUPS) + r
            rows = slice(h * P, (h + 1) * P)
            xT_h = xT[rows, :]
            xdtT_h = xT_h * dtT[h:h + 1, :]
            seg = acsT[h:h + 1, :] - acs[:, h:h + 1]
            mT = (jnp.exp(jnp.where(causalT, seg, -jnp.inf)) * cbT).astype(BF16)
            y_dT = jnp.dot(xdtT_h.astype(BF16), mT, preferred_element_type=F32)
            y_oT = y_offT[r * P:(r + 1) * P, :] * exp_acsT[h:h + 1, :]
            yT_s[rows, :] = y_dT + y_oT + dcol_ref[rows, :] * xT_h
            st_lhs.append((xdtT_h * dec_endT[h:h + 1, :]).astype(BF16))
        st = jnp.dot(jnp.concatenate(st_lhs, axis=0), b_g, preferred_element_type=F32)
        for r in range(SSD_HEADS // SSD_GROUPS):
            h = g * (SSD_HEADS // SSD_GROUPS) + r
            rows = slice(h * P, (h + 1) * P)
            h_s[rows, :] = h_s[rows, :] * chunk_dec[h:h + 1, :] + st[r * P:(r + 1) * P, :]

    _gated_group_norm(yT_s[...].T, u_s[:, C_Z:C_Z + D_MODEL], sng_ref, yc_s)

    gl_s[CONF_HIST_PAD:CONF_HIST_PAD + L, :] = u_s[:, C_GA:C_GA + D_MODEL] * _sigmoid(u_s[:, C_GB:C_GB + D_MODEL])
    _conformer(gl_s, L, u_s[:, C_CG:C_CG + D_MODEL], ccw_ref, ccb_ref, lng_ref, lnb_ref, conv_s, yc_s)

    y_ref[...] = x + jnp.dot(yc_s[...], wo_ref[...], preferred_element_type=F32)

    xbc_s[0:SSD_HIST_PAD, :] = xbc_s[L:L + SSD_HIST_PAD, :]
    gl_s[0:CONF_HIST_PAD, :] = gl_s[L:L + CONF_HIST_PAD, :]

    @pl.when(i == pl.num_programs(0) - 1)
    def _():
        hT_ref[...] = h_s[...]
        sh_ref[...] = xbc_s[0:SSD_HIST_PAD, :]
        chh_ref[...] = gl_s[0:CONF_HIST_PAD, :]


def _l0_prompt(x, p, h0, sh0, ch0):
    S = x.shape[0]
    assert S % CHUNK == 0
    nc = S // CHUNK
    row_blk = lambda w: pl.BlockSpec((CHUNK, w), lambda i: (i, 0))
    in_specs = [row_blk(D_MODEL), _const_spec((1, D_MODEL)), _const_spec((D_MODEL, IN0_PAD)),
                _const_spec((SSD_CONV, SSD_CONV_DIM)), _const_spec((1, SSD_CONV_DIM)),
                _const_spec((1, LANES)), _const_spec((1, LANES)), _const_spec((D_MODEL, 1)), _const_spec((1, D_MODEL)),
                _const_spec((CONF_CONV, D_MODEL)), _const_spec((1, D_MODEL)), _const_spec((1, D_MODEL)),
                _const_spec((1, D_MODEL)), _const_spec((2 * D_MODEL, D_MODEL)),
                _const_spec((D_MODEL, SSD_STATE)), _const_spec((SSD_HIST_PAD, SSD_CONV_DIM)),
                _const_spec((CONF_HIST_PAD, D_MODEL))]
    out_shape = (jax.ShapeDtypeStruct((S, D_MODEL), F32), jax.ShapeDtypeStruct((D_MODEL, SSD_STATE), F32),
                 jax.ShapeDtypeStruct((SSD_HIST_PAD, SSD_CONV_DIM), F32), jax.ShapeDtypeStruct((CONF_HIST_PAD, D_MODEL), F32))
    out_specs = (row_blk(D_MODEL), _const_spec((D_MODEL, SSD_STATE)), _const_spec((SSD_HIST_PAD, SSD_CONV_DIM)),
                 _const_spec((CONF_HIST_PAD, D_MODEL)))
    scratch = [pltpu.VMEM((CHUNK, IN0_PAD), F32), pltpu.VMEM((SSD_HIST_PAD + CHUNK, SSD_CONV_DIM), F32),
               pltpu.VMEM((CONF_HIST_PAD + CHUNK, D_MODEL), F32), pltpu.VMEM((D_MODEL, SSD_STATE), F32),
               pltpu.VMEM((CHUNK, SSD_CONV_DIM), F32), pltpu.VMEM((D_MODEL, CHUNK), F32),
               pltpu.VMEM((CHUNK, D_MODEL), F32), pltpu.VMEM((CHUNK, 2 * D_MODEL), BF16)]
    return pl.pallas_call(
        _l0_prompt_kernel, grid=(nc,), in_specs=in_specs, out_specs=out_specs, out_shape=out_shape,
        scratch_shapes=scratch, name="l0_prompt",
        compiler_params=pltpu.CompilerParams(dimension_semantics=("arbitrary",), vmem_limit_bytes=VMEM_LIMIT),
    )(x, p["ng"], p["w_in"], p["cw"], p["cb"], p["dtb"], p["alog"], p["dcol"], p["sng"],
      p["ccw"], p["ccb"], p["lng"], p["lnb"], p["w_out"], h0, sh0, ch0)


def _norm_proj_kernel(x_ref, g_ref, w_ref, o_ref):
    xn = _rms_rows(x_ref[...], g_ref[...]).astype(BF16)
    o_ref[...] = jnp.dot(xn, w_ref[...], preferred_element_type=F32)


def _norm_proj(x, g, w):
    m, n = x.shape[0], w.shape[1]
    return pl.pallas_call(
        _norm_proj_kernel, grid=(1,),
        in_specs=[_const_spec(x.shape), _const_spec(g.shape), _const_spec(w.shape)],
        out_specs=_const_spec((m, n)), out_shape=jax.ShapeDtypeStruct((m, n), F32), name="norm_proj",
        compiler_params=pltpu.CompilerParams(vmem_limit_bytes=VMEM_LIMIT),
    )(x, g, w)


def _proj_res_kernel(r_ref, a_ref, w_ref, o_ref):
    o_ref[...] = r_ref[...] + jnp.dot(a_ref[...].astype(BF16), w_ref[...], preferred_element_type=F32)


def _proj_res(res, a, w):
    return pl.pallas_call(
        _proj_res_kernel, grid=(1,),
        in_specs=[_const_spec(res.shape), _const_spec(a.shape), _const_spec(w.shape)],
        out_specs=_const_spec(res.shape), out_shape=jax.ShapeDtypeStruct(res.shape, F32), name="proj_res",
        compiler_params=pltpu.CompilerParams(vmem_limit_bytes=VMEM_LIMIT),
    )(res, a, w)


def _l0_sample_kernel(n_tok, u_ref, cw_ref, cb_ref, dtb_ref, alog_ref, dcol_ref, sng_ref,
                      ccw_ref, ccb_ref, lng_ref, lnb_ref, h0_ref, sh0_ref, ch0_ref,
                      yc_ref, hT_ref, sh_ref, chh_ref,
                      xbc_s, gl_s, act_s, pad_s, yT_s, conv_s):
    R = 8
    u = u_ref[0]
    xbc_s[0:SSD_HIST_PAD, :] = sh0_ref[0]
    xbc_s[SSD_HIST_PAD:SSD_HIST_PAD + R, :] = u[:, C_XBC:C_XBC + SSD_CONV_DIM]
    _ssd_conv_silu(xbc_s, cw_ref, cb_ref, R, act_s)

    dt = _softplus(u[:, C_DT:C_DT + LANES] + dtb_ref[...])
    a = dt * (-jnp.exp(alog_ref[...]))
    pad_s[...] = jnp.zeros(pad_s.shape, F32)
    pad_s[0:R, 0:LANES] = dt
    dtT = pad_s[:, 0:LANES].T
    pad_s[0:R, 0:LANES] = a
    decT = jnp.exp(pad_s[:, 0:LANES].T)
    pad_s[0:R, :] = act_s[:, 0:D_MODEL]
    xT = pad_s[...].T

    P = SSD_HEAD_DIM
    hpg = SSD_HEADS // SSD_GROUPS
    yT_s[...] = jnp.zeros(yT_s.shape, F32)
    for h in range(SSD_HEADS):
        g = h // hpg
        rows = slice(h * P, (h + 1) * P)
        hh = h0_ref[0, rows, :]
        for t in range(n_tok):
            b_t = act_s[t:t + 1, D_MODEL + g * SSD_STATE:D_MODEL + (g + 1) * SSD_STATE]
            c_t = act_s[t:t + 1, D_MODEL + (SSD_GROUPS + g) * SSD_STATE:D_MODEL + (SSD_GROUPS + g + 1) * SSD_STATE]
            xdt = xT[rows, t:t + 1] * dtT[h:h + 1, t:t + 1]
            hh = hh * decT[h:h + 1, t:t + 1] + xdt * b_t
            yT_s[rows, t:t + 1] = jnp.sum(hh * c_t, axis=-1, keepdims=True)
        hT_ref[0, rows, :] = hh
    y = (yT_s[...] + dcol_ref[...] * xT).T[0:R, :]
    _gated_group_norm(y, u[:, C_Z:C_Z + D_MODEL], sng_ref, yc_ref.at[0])

    gl_s[0:CONF_HIST_PAD, :] = ch0_ref[0]
    gl_s[CONF_HIST_PAD:CONF_HIST_PAD + R, :] = u[:, C_GA:C_GA + D_MODEL] * _sigmoid(u[:, C_GB:C_GB + D_MODEL])
    _conformer(gl_s, R, u[:, C_CG:C_CG + D_MODEL], ccw_ref, ccb_ref, lng_ref, lnb_ref, conv_s, yc_ref.at[0])

    sh_ref[0] = xbc_s[...]
    chh_ref[0] = gl_s[...]


def _l0_sample(u, p, h0, sh0, ch0, n_tok):
    B = u.shape[0]
    seq_blk = lambda *s: pl.BlockSpec((1,) + s, lambda b: (b,) + (0,) * len(s))
    in_specs = [seq_blk(8, IN0_PAD),
                _const_spec((SSD_CONV, SSD_CONV_DIM)), _const_spec((1, SSD_CONV_DIM)),
                _const_spec((1, LANES)), _const_spec((1, LANES)), _const_spec((D_MODEL, 1)), _const_spec((1, D_MODEL)),
                _const_spec((CONF_CONV, D_MODEL)), _const_spec((1, D_MODEL)), _const_spec((1, D_MODEL)),
                _const_spec((1, D_MODEL)),
                seq_blk(D_MODEL, SSD_STATE), seq_blk(SSD_HIST_PAD, SSD_CONV_DIM), seq_blk(CONF_HIST_PAD, D_MODEL)]
    out_shape = (jax.ShapeDtypeStruct((B, 8, 2 * D_MODEL), F32), jax.ShapeDtypeStruct((B, D_MODEL, SSD_STATE), F32),
                 jax.ShapeDtypeStruct((B, SSD_HIST_PAD + 8, SSD_CONV_DIM), F32),
                 jax.ShapeDtypeStruct((B, CONF_HIST_PAD + 8, D_MODEL), F32))
    out_specs = (seq_blk(8, 2 * D_MODEL), seq_blk(D_MODEL, SSD_STATE), seq_blk(SSD_HIST_PAD + 8, SSD_CONV_DIM),
                 seq_blk(CONF_HIST_PAD + 8, D_MODEL))
    scratch = [pltpu.VMEM((SSD_HIST_PAD + 8, SSD_CONV_DIM), F32), pltpu.VMEM((CONF_HIST_PAD + 8, D_MODEL), F32),
               pltpu.VMEM((8, SSD_CONV_DIM), F32), pltpu.VMEM((LANES, D_MODEL), F32),
               pltpu.VMEM((D_MODEL, LANES), F32), pltpu.VMEM((8, D_MODEL), F32)]
    return pl.pallas_call(
        functools.partial(_l0_sample_kernel, n_tok), grid=(B,), in_specs=in_specs, out_specs=out_specs,
        out_shape=out_shape, scratch_shapes=scratch, name="l0_sample",
        compiler_params=pltpu.CompilerParams(dimension_semantics=("arbitrary",), vmem_limit_bytes=VMEM_LIMIT),
    )(u, p["cw"], p["cb"], p["dtb"], p["alog"], p["dcol"], p["sng"], p["ccw"], p["ccb"], p["lng"], p["lnb"],
      h0, sh0, ch0)


def _prep_l0_params(norm_g, w_in, conv_w, conv_b, dt_bias, a_log, d_skip, ssd_norm_g,
                    cconv_w, cconv_b, cln_g, cln_b, w_out):
    o_dt = D_MODEL + SSD_CONV_DIM
    w = jnp.concatenate([w_in[:, :o_dt], w_in[:, o_dt + SSD_HEADS:], w_in[:, o_dt:o_dt + SSD_HEADS],
                         jnp.zeros((D_MODEL, LANES - SSD_HEADS), w_in.dtype)], axis=1)
    lane_pad = lambda v: jnp.pad(v.astype(F32), (0, LANES - SSD_HEADS)).reshape(1, LANES)
    return dict(ng=norm_g.reshape(1, -1), w_in=w.astype(BF16), cw=conv_w, cb=conv_b.reshape(1, -1),
                dtb=lane_pad(dt_bias), alog=lane_pad(a_log),
                dcol=jnp.repeat(d_skip.astype(F32), SSD_HEAD_DIM).reshape(-1, 1), sng=ssd_norm_g.reshape(1, -1),
                ccw=cconv_w, ccb=cconv_b.reshape(1, -1), lng=cln_g.reshape(1, -1), lnb=cln_b.reshape(1, -1),
                w_out=w_out.astype(BF16))


def _hybrid_layer(x_p, x_s, state_ssm, ssd_hist, conf_hist, p):
    S = x_p.shape[1]
    B, T, _ = x_s.shape
    H, P, N = SSD_HEADS, SSD_HEAD_DIM, SSD_STATE
    zeros = lambda *s: jnp.zeros(s, F32)
    yp, h_p, sh_p, ch_p = _l0_prompt(x_p.reshape(S, D_MODEL), p, zeros(D_MODEL, N), zeros(SSD_HIST_PAD, SSD_CONV_DIM),
                                     zeros(CONF_HIST_PAD, D_MODEL))
    xs2 = x_s.reshape(B * T, D_MODEL)
    u_s = _norm_proj(xs2, p["ng"], p["w_in"]).reshape(B, T, IN0_PAD)
    u_s = jnp.pad(u_s, ((0, 0), (0, 8 - T), (0, 0)))
    sh0 = jnp.pad(ssd_hist, ((0, 0), (SSD_HIST_PAD - (SSD_CONV - 1), 0), (0, 0)))
    ch0 = jnp.pad(conf_hist, ((0, 0), (CONF_HIST_PAD - (CONF_CONV - 1), 0), (0, 0)))
    yc_s, h_s, sh_s, ch_s = _l0_sample(u_s, p, state_ssm.reshape(B, H * P, N), sh0, ch0, T)
    ys = _proj_res(xs2, yc_s[:, :T].reshape(B * T, 2 * D_MODEL), p["w_out"])
    return (yp.reshape(1, S, D_MODEL), ys.reshape(B, T, D_MODEL),
            h_p.reshape(1, H, P, N), h_s.reshape(B, H, P, N),
            sh_p[None, SSD_HIST_PAD - (SSD_CONV - 1):], sh_s[:, SSD_HIST_PAD + T - (SSD_CONV - 1):SSD_HIST_PAD + T],
            ch_p[None, CONF_HIST_PAD - (CONF_CONV - 1):], ch_s[:, CONF_HIST_PAD + T - (CONF_CONV - 1):CONF_HIST_PAD + T])


ATT_WIDTH = ATT_HEADS * HEAD_DIM
R_Q, R_K, R_V, R_G, IN1_ROWS = 0, ATT_WIDTH, ATT_WIDTH + KV_WIDTH, ATT_WIDTH + 2 * KV_WIDTH, 2 * ATT_WIDTH + 2 * KV_WIDTH
SCALE = HEAD_DIM ** -0.5


def _head_rms_cols(xT, g_col):
    out = []
    for h in range(xT.shape[0] // HEAD_DIM):
        xh = xT[h * HEAD_DIM:(h + 1) * HEAD_DIM, :]
        r = lax.rsqrt(jnp.mean(xh * xh, axis=0, keepdims=True) + NORM_EPS)
        out.append(xh * r * g_col)
    return out


def _l1_proj_kernel(x_ref, ng_ref, wT_ref, qg_ref, kg_ref, qT_ref, gT_ref, kn_ref, vn_ref, kb_ref, vT_ref, km_ref):
    xn = _rms_rows(x_ref[...], ng_ref[...]).astype(BF16)
    uT = lax.dot_general(wT_ref[...], xn, _NT, preferred_element_type=F32)
    for h, qh in enumerate(_head_rms_cols(uT[R_Q:R_K, :], qg_ref[...])):
        qT_ref[h * HEAD_DIM:(h + 1) * HEAD_DIM, :] = (qh * SCALE).astype(BF16)
    k_nat = jnp.concatenate(_head_rms_cols(uT[R_K:R_V, :], kg_ref[...]), axis=0).T
    kn_ref[...] = k_nat
    kb_ref[0] = k_nat.astype(BF16)
    km_ref[0] = jnp.mean(k_nat, axis=0, keepdims=True)
    vT = uT[R_V:R_G, :]
    vn_ref[...] = vT.T
    vT_ref[0] = vT.astype(BF16)
    gT_ref[...] = _silu(uT[R_G:IN1_ROWS, :])


def _l1_proj(x, ng, wT, qg_col, kg_col):
    S = x.shape[0]
    nb = S // CHUNK
    col_blk = lambda r: pl.BlockSpec((r, CHUNK), lambda i: (0, i))
    row_blk = lambda w: pl.BlockSpec((CHUNK, w), lambda i: (i, 0))
    blk3 = lambda a, b: pl.BlockSpec((1, a, b), lambda i: (i, 0, 0))
    out_shape = (jax.ShapeDtypeStruct((ATT_WIDTH, S), BF16), jax.ShapeDtypeStruct((ATT_WIDTH, S), F32),
                 jax.ShapeDtypeStruct((S, KV_WIDTH), F32), jax.ShapeDtypeStruct((S, KV_WIDTH), F32),
                 jax.ShapeDtypeStruct((nb, CHUNK, KV_WIDTH), BF16), jax.ShapeDtypeStruct((nb, KV_WIDTH, CHUNK), BF16),
                 jax.ShapeDtypeStruct((nb, 1, KV_WIDTH), F32))
    out_specs = (col_blk(ATT_WIDTH), col_blk(ATT_WIDTH), row_blk(KV_WIDTH), row_blk(KV_WIDTH),
                 blk3(CHUNK, KV_WIDTH), blk3(KV_WIDTH, CHUNK), blk3(1, KV_WIDTH))
    return pl.pallas_call(
        _l1_proj_kernel, grid=(nb,),
        in_specs=[row_blk(D_MODEL), _const_spec((1, D_MODEL)), _const_spec((IN1_ROWS, D_MODEL)),
                  _const_spec((HEAD_DIM, 1)), _const_spec((HEAD_DIM, 1))],
        out_specs=out_specs, out_shape=out_shape, name="l1_proj",
        compiler_params=pltpu.CompilerParams(dimension_semantics=("arbitrary",), vmem_limit_bytes=VMEM_LIMIT),
    )(x, ng, wT, qg_col, kg_col)


def _top_k_bias(gate, idx, n_valid_mask, axis):
    n = gate.shape[axis]
    g = jnp.where(n_valid_mask, gate, -jnp.inf)
    sel = jnp.zeros(gate.shape, F32)
    for _ in range(MOBA_TOP_K):
        mx = jnp.max(g, axis=axis, keepdims=True)
        first = jnp.min(jnp.where(g == mx, idx, n), axis=axis, keepdims=True)
        pick = jnp.logical_and(idx == first, mx > -jnp.inf)
        sel = jnp.where(pick, 1.0, sel)
        g = jnp.where(pick, -jnp.inf, g)
    return jnp.where(sel > 0.0, 0.0, MASKED)


def _attn_prompt_kernel(qT_ref, gT_ref, r_ref, kb_ref, vT_ref, km_ref, wo_ref, o_ref, qz_s, bias_s, m_s, l_s, acc_s):
    t = pl.program_id(0)
    L = CHUNK
    nb = km_ref.shape[0]
    gsz = ATT_HEADS // KV_HEADS
    hd = HEAD_DIM

    zeros = jnp.zeros((hd, L), BF16)
    for hq in range(ATT_HEADS):
        q = qT_ref[hq * hd:(hq + 1) * hd, :]
        qz_s[hq] = jnp.concatenate([q, zeros] if (hq // gsz) % 2 == 0 else [zeros, q], axis=0)

    n_i = lax.broadcasted_iota(jnp.int32, (nb, L), 0)
    for hq in range(ATT_HEADS):
        j = (hq // gsz) // 2
        gate = jnp.dot(km_ref[:, j * LANES:(j + 1) * LANES].astype(BF16), qz_s[hq], preferred_element_type=F32)
        bias_s[hq] = _top_k_bias(gate, n_i, n_i < t, 0)

    def scores(hq, n):
        j = (hq // gsz) // 2
        return jnp.dot(kb_ref[n, :, j * LANES:(j + 1) * LANES], qz_s[hq], preferred_element_type=F32)

    def v_rows(hq, n):
        kvh = hq // gsz
        return vT_ref[n, kvh * hd:(kvh + 1) * hd, :]

    causalT = lax.broadcasted_iota(jnp.int32, (L, L), 0) <= lax.broadcasted_iota(jnp.int32, (L, L), 1)
    for hq in range(ATT_HEADS):
        s = jnp.where(causalT, scores(hq, t), MASKED)
        m = jnp.max(s, axis=0, keepdims=True)
        p = jnp.exp(s - m)
        m_s[hq] = m
        l_s[hq] = jnp.sum(p, axis=0, keepdims=True)
        acc_s[hq * hd:(hq + 1) * hd, :] = jnp.dot(v_rows(hq, t), p.astype(BF16), preferred_element_type=F32)

    def past_block(n, carry):
        for hq in range(ATT_HEADS):
            s = scores(hq, n) + bias_s[hq, pl.ds(n, 1), :]
            m_old = m_s[hq]
            m_new = jnp.maximum(m_old, jnp.max(s, axis=0, keepdims=True))
            p = jnp.exp(s - m_new)
            alpha = jnp.exp(m_old - m_new)
            m_s[hq] = m_new
            l_s[hq] = alpha * l_s[hq] + jnp.sum(p, axis=0, keepdims=True)
            rows = slice(hq * hd, (hq + 1) * hd)
            acc_s[rows, :] = alpha * acc_s[rows, :] + jnp.dot(v_rows(hq, n), p.astype(BF16), preferred_element_type=F32)
        return carry

    lax.fori_loop(0, t, past_block, 0)

    for hq in range(ATT_HEADS):
        rows = slice(hq * hd, (hq + 1) * hd)
        acc_s[rows, :] = acc_s[rows, :] / l_s[hq] * gT_ref[rows, :]
    og = acc_s[...].T.astype(BF16)
    o_ref[...] = r_ref[...] + jnp.dot(og, wo_ref[...], preferred_element_type=F32)


def _attn_prompt(qT, gT, res, kb, vT, km, wo):
    S = res.shape[0]
    nb = S // CHUNK
    col_blk = pl.BlockSpec((ATT_WIDTH, CHUNK), lambda i: (0, i))
    row_blk = pl.BlockSpec((CHUNK, D_MODEL), lambda i: (i, 0))
    scratch = [pltpu.VMEM((ATT_HEADS, 2 * HEAD_DIM, CHUNK), BF16), pltpu.VMEM((ATT_HEADS, nb, CHUNK), F32),
               pltpu.VMEM((ATT_HEADS, 1, CHUNK), F32), pltpu.VMEM((ATT_HEADS, 1, CHUNK), F32),
               pltpu.VMEM((ATT_WIDTH, CHUNK), F32)]
    return pl.pallas_call(
        _attn_prompt_kernel, grid=(nb,),
        in_specs=[col_blk, col_blk, row_blk, _const_spec(kb.shape), _const_spec(vT.shape), _const_spec(km.shape),
                  _const_spec(wo.shape)],
        out_specs=row_blk, out_shape=jax.ShapeDtypeStruct((S, D_MODEL), F32), scratch_shapes=scratch,
        name="attn_prompt",
        compiler_params=pltpu.CompilerParams(dimension_semantics=("arbitrary",), vmem_limit_bytes=VMEM_LIMIT),
    )(qT, gT, res, kb, vT, km, wo)


def _pair_rms(x, g2):
    lo = lax.broadcasted_iota(jnp.int32, (x.shape[0], LANES), 1) < HEAD_DIM
    out = []
    for c in range(0, x.shape[1], LANES):
        xt = x[:, c:c + LANES]
        sq = xt * xt
        s_lo = jnp.sum(jnp.where(lo, sq, 0.0), axis=-1, keepdims=True)
        s_hi = jnp.sum(jnp.where(lo, 0.0, sq), axis=-1, keepdims=True)
        r = jnp.where(lo, lax.rsqrt(s_lo / HEAD_DIM + NORM_EPS), lax.rsqrt(s_hi / HEAD_DIM + NORM_EPS))
        out.append(xt * r * g2)
    return jnp.concatenate(out, axis=-1)


def _l1_proj_nat_kernel(x_ref, ng_ref, w_ref, qg_ref, kg_ref, q_ref, k_ref, v_ref, g_ref):
    xn = _rms_rows(x_ref[...], ng_ref[...]).astype(BF16)
    u = jnp.dot(xn, w_ref[...], preferred_element_type=F32)
    q_ref[...] = _pair_rms(u[:, R_Q:R_K], qg_ref[...]) * SCALE
    k_ref[...] = _pair_rms(u[:, R_K:R_V], kg_ref[...])
    v_ref[...] = u[:, R_V:R_G]
    g_ref[...] = _silu(u[:, R_G:IN1_ROWS])


def _l1_proj_nat(x, ng, w, qg2, kg2):
    m = x.shape[0]
    shapes = ((m, ATT_WIDTH), (m, KV_WIDTH), (m, KV_WIDTH), (m, ATT_WIDTH))
    return pl.pallas_call(
        _l1_proj_nat_kernel, grid=(1,),
        in_specs=[_const_spec(x.shape), _const_spec(ng.shape), _const_spec(w.shape), _const_spec(qg2.shape),
                  _const_spec(kg2.shape)],
        out_specs=tuple(_const_spec(s) for s in shapes),
        out_shape=tuple(jax.ShapeDtypeStruct(s, F32) for s in shapes), name="l1_proj_nat",
        compiler_params=pltpu.CompilerParams(vmem_limit_bytes=VMEM_LIMIT),
    )(x, ng, w, qg2, kg2)


def _attn_sample_kernel(n_tok, pt_ref, q_ref, kn_ref, vn_ref, ka_ref, kb_ref, va_ref, vb_ref, o_ref,
                        qz_s, new_s, km_s, m_s, l_s, o_s):
    n = pl.program_id(1)
    nblk = pl.num_programs(1)
    NB = km_s.shape[0]
    R = 8
    gsz = ATT_HEADS // KV_HEADS
    HR = ATT_HEADS // 2 * R
    lane = lax.broadcasted_iota(jnp.int32, (R, LANES), 1)

    @pl.when(n == 0)
    def _():
        q = q_ref[0]
        for hq in range(ATT_HEADS):
            kvh = hq // gsz
            j = kvh // 2
            piece = q[:, (hq // 2) * LANES:(hq // 2 + 1) * LANES]
            if hq % 2 != kvh % 2:
                piece = pltpu.roll(piece, HEAD_DIM, 1)
            piece = jnp.where((lane >= HEAD_DIM) == (kvh % 2 == 1), piece, 0.0)
            r0 = (hq - j * (ATT_HEADS // 2)) * R
            qz_s[j, r0:r0 + R, :] = piece

    def partial_softmax(j, s, v_tile, slot):
        m = jnp.max(s, axis=-1, keepdims=True)
        p = jnp.exp(s - m)
        m_s[slot, j] = jnp.broadcast_to(m, (HR, LANES))
        l_s[slot, j] = jnp.broadcast_to(jnp.sum(p, axis=-1, keepdims=True), (HR, LANES))
        o_s[slot, j] = jnp.dot(p.astype(BF16), v_tile, preferred_element_type=F32)

    kblk = jnp.concatenate([ka_ref[0], kb_ref[0]], axis=0)
    vblk = jnp.concatenate([va_ref[0], vb_ref[0]], axis=0)
    km_s[pl.ds(n, 1), :] = jnp.mean(kblk, axis=0, keepdims=True)
    for j in range(2):
        kt = kblk[:, j * LANES:(j + 1) * LANES].astype(BF16)
        s = lax.dot_general(qz_s[j].astype(BF16), kt, _NT, preferred_element_type=F32)
        partial_softmax(j, s, vblk[:, j * LANES:(j + 1) * LANES].astype(BF16), n)

    @pl.when(n == nblk - 1)
    def _():
        new_s[...] = jnp.zeros(new_s.shape, F32)
        new_s[0, 0:R, :] = kn_ref[0]
        new_s[1, 0:R, :] = vn_ref[0]
        row_tok = lax.broadcasted_iota(jnp.int32, (HR, LANES), 0) % R
        key = lax.broadcasted_iota(jnp.int32, (HR, LANES), 1)
        own_ok = jnp.logical_and(key <= row_tok, key < n_tok)
        blk_i = lax.broadcasted_iota(jnp.int32, (HR, NB), 1)
        lo = lane < HEAD_DIM
        for j in range(2):
            qj = qz_s[j].astype(BF16)
            kt = new_s[0, :, j * LANES:(j + 1) * LANES].astype(BF16)
            s = lax.dot_general(qj, kt, _NT, preferred_element_type=F32)
            partial_softmax(j, jnp.where(own_ok, s, MASKED), new_s[1, :, j * LANES:(j + 1) * LANES].astype(BF16), NB)

            gate = lax.dot_general(qj, km_s[:, j * LANES:(j + 1) * LANES].astype(BF16), _NT,
                                   preferred_element_type=F32)
            bias = _top_k_bias(gate, blk_i, blk_i >= 0, 1)
            m_tot = m_s[NB, j]
            for b in range(NB):
                m_tot = jnp.maximum(m_tot, m_s[b, j] + bias[:, b:b + 1])
            w = jnp.exp(m_s[NB, j] - m_tot)
            num = w * o_s[NB, j]
            den = w * l_s[NB, j]
            for b in range(NB):
                w = jnp.exp(m_s[b, j] + bias[:, b:b + 1] - m_tot)
                num = num + w * o_s[b, j]
                den = den + w * l_s[b, j]
            oj = num / den
            for pair in range(ATT_HEADS // 4):
                pieces = []
                for hq in (j * (ATT_HEADS // 2) + 2 * pair, j * (ATT_HEADS // 2) + 2 * pair + 1):
                    kvh = hq // gsz
                    r0 = (hq - j * (ATT_HEADS // 2)) * R
                    piece = oj[r0:r0 + R, :]
                    if hq % 2 != kvh % 2:
                        piece = pltpu.roll(piece, HEAD_DIM, 1)
                    pieces.append(piece)
                tile = j * (ATT_HEADS // 4) + pair
                o_ref[0, :, tile * LANES:(tile + 1) * LANES] = jnp.where(lo, pieces[0], pieces[1])


def _attn_sample(page_table, q8, kn8, vn8, cache_k, cache_v, n_tok):
    B, n_pages = page_table.shape
    ppb = CHUNK // PAGE
    assert n_pages % ppb == 0, "the new tokens' block is assumed to hold no cached keys"
    nblk = n_pages // ppb
    HR = ATT_HEADS // 2 * 8
    seq_blk = lambda w: pl.BlockSpec((1, 8, w), lambda b, n, pt: (b, 0, 0))
    page_blk = lambda off: pl.BlockSpec((1, PAGE, KV_WIDTH), lambda b, n, pt: (pt[b, ppb * n + off], 0, 0))
    grid_spec = pltpu.PrefetchScalarGridSpec(
        num_scalar_prefetch=1, grid=(B, nblk),
        in_specs=[seq_blk(ATT_WIDTH), seq_blk(KV_WIDTH), seq_blk(KV_WIDTH), page_blk(0), page_blk(1), page_blk(0), page_blk(1)],
        out_specs=seq_blk(ATT_WIDTH),
        scratch_shapes=[pltpu.VMEM((2, HR, LANES), F32), pltpu.VMEM((2, LANES, KV_WIDTH), F32),
                        pltpu.VMEM((nblk, KV_WIDTH), F32), pltpu.VMEM((nblk + 1, 2, HR, LANES), F32),
                        pltpu.VMEM((nblk + 1, 2, HR, LANES), F32), pltpu.VMEM((nblk + 1, 2, HR, LANES), F32)])
    return pl.pallas_call(
        functools.partial(_attn_sample_kernel, n_tok), grid_spec=grid_spec,
        out_shape=jax.ShapeDtypeStruct((B, 8, ATT_WIDTH), F32), name="attn_sample",
        compiler_params=pltpu.CompilerParams(dimension_semantics=("arbitrary", "arbitrary"), vmem_limit_bytes=VMEM_LIMIT),
    )(page_table, q8, kn8, vn8, cache_k, cache_k, cache_v, cache_v)


def _gated_proj_res_kernel(r_ref, a_ref, g_ref, w_ref, o_ref):
    og = (a_ref[...] * g_ref[...]).astype(BF16)
    o_ref[...] = r_ref[...] + jnp.dot(og, w_ref[...], preferred_element_type=F32)


def _gated_proj_res(res, a, g, w):
    return pl.pallas_call(
        _gated_proj_res_kernel, grid=(1,),
        in_specs=[_const_spec(res.shape), _const_spec(a.shape), _const_spec(g.shape), _const_spec(w.shape)],
        out_specs=_const_spec(res.shape), out_shape=jax.ShapeDtypeStruct(res.shape, F32), name="gated_proj_res",
        compiler_params=pltpu.CompilerParams(vmem_limit_bytes=VMEM_LIMIT),
    )(res, a, g, w)


def _attention_layer(yp, ys, cache_k, cache_v, page_table, norm_g, w_in, qn_g, kn_g, w_out):
    S = yp.shape[1]
    B, T, _ = ys.shape
    ng = norm_g.reshape(1, -1)
    wo = w_out.astype(BF16)
    qT, gT, k_p, v_p, kb, vT, km = _l1_proj(yp.reshape(S, D_MODEL), ng, w_in.T.astype(BF16),
                                            qn_g.reshape(-1, 1), kn_g.reshape(-1, 1))
    yp2 = _attn_prompt(qT, gT, yp.reshape(S, D_MODEL), kb, vT, km.reshape(-1, KV_WIDTH), wo)

    ys2 = ys.reshape(B * T, D_MODEL)
    pair = lambda g: jnp.concatenate([g, g]).reshape(1, LANES)
    q_s, k_s, v_s, g_s = _l1_proj_nat(ys2, ng, w_in.astype(BF16), pair(qn_g), pair(kn_g))
    pad_tok = lambda a: jnp.pad(a.reshape(B, T, -1), ((0, 0), (0, 8 - T), (0, 0)))
    o_s = _attn_sample(page_table, pad_tok(q_s), pad_tok(k_s), pad_tok(v_s),
                       cache_k.reshape(cache_k.shape[0], PAGE, KV_WIDTH), cache_v.reshape(cache_v.shape[0], PAGE, KV_WIDTH), T)
    ys3 = _gated_proj_res(ys2, o_s[:, :T].reshape(B * T, ATT_WIDTH), g_s, wo)
    return (yp2.reshape(1, S, D_MODEL), ys3.reshape(B, T, D_MODEL),
            k_p.reshape(1, S, KV_HEADS, HEAD_DIM), v_p.reshape(1, S, KV_HEADS, HEAD_DIM),
            k_s.reshape(B, T, KV_HEADS, HEAD_DIM), v_s.reshape(B, T, KV_HEADS, HEAD_DIM))


def kernel(x_prompt, x_sample, state_ssm, state_ssd_conv, state_conf_conv, cache_k, cache_v, page_table, norm0_g, w_in0, ssd_conv_w, ssd_conv_b, ssd_dt_bias, ssd_a_log, ssd_d, ssd_norm_g, conf_conv_w, conf_conv_b, conf_ln_g, conf_ln_b, w_out0, norm1_g, w_in1, q_norm_g, k_norm_g, w_out1):
    p0 = _prep_l0_params(norm0_g[0], w_in0[0], ssd_conv_w[0], ssd_conv_b[0], ssd_dt_bias[0], ssd_a_log[0], ssd_d[0],
                         ssd_norm_g[0], conf_conv_w[0], conf_conv_b[0], conf_ln_g[0], conf_ln_b[0], w_out0[0])
    yp, ys, h_p, h_s, sh_p, sh_s, ch_p, ch_s = _hybrid_layer(
        x_prompt, x_sample, state_ssm[0], state_ssd_conv[0], state_conf_conv[0], p0)
    yp, ys, k_p, v_p, k_s, v_s = _attention_layer(yp, ys, cache_k[0], cache_v[0], page_table, norm1_g[0], w_in1[0],
                                                  q_norm_g[0], k_norm_g[0], w_out1[0])
    return (yp, ys, h_p[None], h_s[None], sh_p[None], sh_s[None], ch_p[None], ch_s[None],
            k_p[None], v_p[None], k_s[None], v_s[None])
```

```python
import functools

import jax
import jax.numpy as jnp
from jax import lax
from jax.experimental import pallas as pl
from jax.experimental.pallas import tpu as pltpu

F32 = jnp.float32
BF16 = jnp.bfloat16
NORM_EPS = 1e-6
MASKED = -1e30
LANES = 128
CHUNK = 256
D_MODEL = 1024
SSD_HEADS = 16
SSD_HEAD_DIM = 64
SSD_GROUPS = 4
SSD_STATE = 128
SSD_CONV = 4
SSD_CONV_DIM = 2048
CONF_CONV = 31
ATT_HEADS = 16
KV_HEADS = 4
HEAD_DIM = 64
KV_WIDTH = KV_HEADS * HEAD_DIM
MOBA_TOP_K = 3
PAGE = 128
C_Z, C_XBC, C_GA, C_GB, C_CG, C_DT, IN0_PAD = 0, 1024, 3072, 4096, 5120, 6144, 6272
SSD_HIST_PAD = 8
CONF_HIST_PAD = 32
SAMPLE_BLOCKS_PER_STEP = 8
VMEM_LIMIT = 56 * 1024 * 1024

_NT = (((1,), (1,)), ((), ()))


def _sigmoid(x):
    return 1.0 / (1.0 + jnp.exp(-x))


def _silu(x):
    return x * _sigmoid(x)


def _softplus(x):
    return jnp.maximum(x, 0.0) + jnp.log1p(jnp.exp(-jnp.abs(x)))


def _rms_rows(x, g):
    ms = jnp.mean(x * x, axis=-1, keepdims=True)
    return x * lax.rsqrt(ms + NORM_EPS) * g


def _const_spec(shape):
    nd = len(shape)
    return pl.BlockSpec(shape, lambda *_: (0,) * nd)


def _ssd_conv_silu(buf_ref, cw_ref, cb_ref, rows, out_ref):
    for c in range(0, SSD_CONV_DIM, 512):
        acc = cb_ref[:, c:c + 512]
        for k in range(SSD_CONV):
            o = SSD_HIST_PAD - (SSD_CONV - 1) + k
            acc = acc + cw_ref[k:k + 1, c:c + 512] * buf_ref[o:o + rows, c:c + 512]
        out_ref[:, c:c + 512] = _silu(acc)


def _gated_group_norm(y, z, sng_ref, yc_ref):
    y = y * _silu(z)
    gw = D_MODEL // SSD_GROUPS
    for g in range(SSD_GROUPS):
        yg = y[:, g * gw:(g + 1) * gw]
        r = lax.rsqrt(jnp.mean(yg * yg, axis=-1, keepdims=True) + NORM_EPS)
        yc_ref[:, g * gw:(g + 1) * gw] = (yg * r * sng_ref[:, g * gw:(g + 1) * gw]).astype(yc_ref.dtype)


def _conformer(gl_ref, rows, cgate, ccw_ref, ccb_ref, lng_ref, lnb_ref, conv_ref, yc_ref):
    for c in range(0, D_MODEL, 256):
        acc = ccb_ref[:, c:c + 256]
        for k in range(CONF_CONV):
            o = CONF_HIST_PAD - (CONF_CONV - 1) + k
            acc = acc + ccw_ref[k:k + 1, c:c + 256] * gl_ref[o:o + rows, c:c + 256]
        conv_ref[:, c:c + 256] = acc
    cv = conv_ref[...]
    mu = jnp.mean(cv, axis=-1, keepdims=True)
    xc = cv - mu
    var = jnp.mean(xc * xc, axis=-1, keepdims=True)
    cn = xc * lax.rsqrt(var + NORM_EPS) * lng_ref[...] + lnb_ref[...]
    yc_ref[:, D_MODEL:2 * D_MODEL] = (_silu(cn) * _silu(cgate)).astype(yc_ref.dtype)


def _l0_prompt_kernel(x_ref, ng_ref, w_ref, cw_ref, cb_ref, dtb_ref, alog_ref, dcol_ref, sng_ref,
                      ccw_ref, ccb_ref, lng_ref, lnb_ref, wo_ref, h0_ref, sh0_ref, ch0_ref,
                      y_ref, hT_ref, sh_ref, chh_ref,
                      u_s, xbc_s, gl_s, h_s, act_s, yT_s, conv_s, yc_s):
    i = pl.program_id(0)
    L = CHUNK

    @pl.when(i == 0)
    def _():
        h_s[...] = h0_ref[...]
        xbc_s[0:SSD_HIST_PAD, :] = sh0_ref[...]
        gl_s[0:CONF_HIST_PAD, :] = ch0_ref[...]

    x = x_ref[...]
    xn = _rms_rows(x, ng_ref[...]).astype(BF16)
    nblk = IN0_PAD // 896
    for c in range(nblk):
        u_s[:, c * 896:(c + 1) * 896] = jnp.dot(xn, w_ref[:, c * 896:(c + 1) * 896], preferred_element_type=F32)

    xbc_s[SSD_HIST_PAD:SSD_HIST_PAD + L, :] = u_s[:, C_XBC:C_XBC + SSD_CONV_DIM]
    _ssd_conv_silu(xbc_s, cw_ref, cb_ref, L, act_s)

    dt = _softplus(u_s[:, C_DT:C_DT + LANES] + dtb_ref[...])
    a = dt * (-jnp.exp(alog_ref[...]))
    r_i = lax.broadcasted_iota(jnp.int32, (L, L), 0)
    c_i = lax.broadcasted_iota(jnp.int32, (L, L), 1)
    tri = (c_i <= r_i).astype(F32)
    acs = jnp.dot(tri, a, precision=lax.Precision.HIGHEST, preferred_element_type=F32)
    acsT = acs.T
    dtT = dt.T
    last = acsT[:, L - 1:L]
    exp_acsT = jnp.exp(acsT)
    dec_endT = jnp.exp(last - acsT)
    chunk_dec = jnp.exp(last)
    causalT = r_i <= c_i

    xT = act_s[:, 0:D_MODEL].T
    P = SSD_HEAD_DIM
    gp = SSD_HEADS // SSD_GROUPS * P
    for g in range(SSD_GROUPS):
        b_g = act_s[:, D_MODEL + g * SSD_STATE:D_MODEL + (g + 1) * SSD_STATE].astype(BF16)
        c_g = act_s[:, D_MODEL + (SSD_GROUPS + g) * SSD_STATE:D_MODEL + (SSD_GROUPS + g + 1) * SSD_STATE].astype(BF16)
        cbT = lax.dot_general(b_g, c_g, _NT, preferred_element_type=F32)
        h_g = h_s[g * gp:(g + 1) * gp, :]
        y_offT = lax.dot_general(h_g.astype(BF16), c_g, _NT, preferred_element_type=F32)
        st_lhs = []
        for r in range(SSD_HEADS // SSD_GROUPS):
            h = g * (SSD_HEADS // SSD_GROUPS) + r
            rows = slice(h * P, (h + 1) * P)
            xT_h = xT[rows, :]
            xdtT_h = xT_h * dtT[h:h + 1, :]
            seg = acsT[h:h + 1, :] - acs[:, h:h + 1]
            mT = (jnp.exp(jnp.where(causalT, seg, -jnp.inf)) * cbT).astype(BF16)
            y_dT = jnp.dot(xdtT_h.astype(BF16), mT, preferred_element_type=F32)
            y_oT = y_offT[r * P:(r + 1) * P, :] * exp_acsT[h:h + 1, :]
            yT_s[rows, :] = y_dT + y_oT + dcol_ref[rows, :] * xT_h
            st_lhs.append((xdtT_h * dec_endT[h:h + 1, :]).astype(BF16))
        st = jnp.dot(jnp.concatenate(st_lhs, axis=0), b_g, preferred_element_type=F32)
        for r in range(SSD_HEADS // SSD_GROUPS):
            h = g * (SSD_HEADS // SSD_GROUPS) + r
            rows = slice(h * P, (h + 1) * P)
            h_s[rows, :] = h_s[rows, :] * chunk_dec[h:h + 1, :] + st[r * P:(r + 1) * P, :]

    _gated_group_norm(yT_s[...].T, u_s[:, C_Z:C_Z + D_MODEL], sng_ref, yc_s)

    gl_s[CONF_HIST_PAD:CONF_HIST_PAD + L, :] = u_s[:, C_GA:C_GA + D_MODEL] * _sigmoid(u_s[:, C_GB:C_GB + D_MODEL])
    _conformer(gl_s, L, u_s[:, C_CG:C_CG + D_MODEL], ccw_ref, ccb_ref, lng_ref, lnb_ref, conv_s, yc_s)

    y_ref[...] = x + jnp.dot(yc_s[...], wo_ref[...], preferred_element_type=F32)

    xbc_s[0:SSD_HIST_PAD, :] = xbc_s[L:L + SSD_HIST_PAD, :]
    gl_s[0:CONF_HIST_PAD, :] = gl_s[L:L + CONF_HIST_PAD, :]

    @pl.when(i == pl.num_programs(0) - 1)
    def _():
        hT_ref[...] = h_s[...]
        sh_ref[...] = xbc_s[0:SSD_HIST_PAD, :]
        chh_ref[...] = gl_s[0:CONF_HIST_PAD, :]


def _l0_prompt(x, p, h0, sh0, ch0):
    S = x.shape[0]
    assert S % CHUNK == 0
    nc = S // CHUNK
    row_blk = lambda w: pl.BlockSpec((CHUNK, w), lambda i: (i, 0))
    in_specs = [row_blk(D_MODEL), _const_spec((1, D_MODEL)), _const_spec((D_MODEL, IN0_PAD)),
                _const_spec((SSD_CONV, SSD_CONV_DIM)), _const_spec((1, SSD_CONV_DIM)),
                _const_spec((1, LANES)), _const_spec((1, LANES)), _const_spec((D_MODEL, 1)), _const_spec((1, D_MODEL)),
                _const_spec((CONF_CONV, D_MODEL)), _const_spec((1, D_MODEL)), _const_spec((1, D_MODEL)),
                _const_spec((1, D_MODEL)), _const_spec((2 * D_MODEL, D_MODEL)),
                _const_spec((D_MODEL, SSD_STATE)), _const_spec((SSD_HIST_PAD, SSD_CONV_DIM)),
                _const_spec((CONF_HIST_PAD, D_MODEL))]
    out_shape = (jax.ShapeDtypeStruct((S, D_MODEL), F32), jax.ShapeDtypeStruct((D_MODEL, SSD_STATE), F32),
                 jax.ShapeDtypeStruct((SSD_HIST_PAD, SSD_CONV_DIM), F32), jax.ShapeDtypeStruct((CONF_HIST_PAD, D_MODEL), F32))
    out_specs = (row_blk(D_MODEL), _const_spec((D_MODEL, SSD_STATE)), _const_spec((SSD_HIST_PAD, SSD_CONV_DIM)),
                 _const_spec((CONF_HIST_PAD, D_MODEL)))
    scratch = [pltpu.VMEM((CHUNK, IN0_PAD), F32), pltpu.VMEM((SSD_HIST_PAD + CHUNK, SSD_CONV_DIM), F32),
               pltpu.VMEM((CONF_HIST_PAD + CHUNK, D_MODEL), F32), pltpu.VMEM((D_MODEL, SSD_STATE), F32),
               pltpu.VMEM((CHUNK, SSD_CONV_DIM), F32), pltpu.VMEM((D_MODEL, CHUNK), F32),
               pltpu.VMEM((CHUNK, D_MODEL), F32), pltpu.VMEM((CHUNK, 2 * D_MODEL), BF16)]
    return pl.pallas_call(
        _l0_prompt_kernel, grid=(nc,), in_specs=in_specs, out_specs=out_specs, out_shape=out_shape,
        scratch_shapes=scratch, name="l0_prompt",
        compiler_params=pltpu.CompilerParams(dimension_semantics=("arbitrary",), vmem_limit_bytes=VMEM_LIMIT),
    )(x, p["ng"], p["w_in"], p["cw"], p["cb"], p["dtb"], p["alog"], p["dcol"], p["sng"],
      p["ccw"], p["ccb"], p["lng"], p["lnb"], p["w_out"], h0, sh0, ch0)


def _norm_proj_kernel(x_ref, g_ref, w_ref, o_ref):
    xn = _rms_rows(x_ref[...], g_ref[...]).astype(BF16)
    o_ref[...] = jnp.dot(xn, w_ref[...], preferred_element_type=F32)


def _norm_proj(x, g, w):
    m, n = x.shape[0], w.shape[1]
    return pl.pallas_call(
        _norm_proj_kernel, grid=(1,),
        in_specs=[_const_spec(x.shape), _const_spec(g.shape), _const_spec(w.shape)],
        out_specs=_const_spec((m, n)), out_shape=jax.ShapeDtypeStruct((m, n), F32), name="norm_proj",
        compiler_params=pltpu.CompilerParams(vmem_limit_bytes=VMEM_LIMIT),
    )(x, g, w)


def _proj_res_kernel(r_ref, a_ref, w_ref, o_ref):
    o_ref[...] = r_ref[...] + jnp.dot(a_ref[...].astype(BF16), w_ref[...], preferred_element_type=F32)


def _proj_res(res, a, w):
    return pl.pallas_call(
        _proj_res_kernel, grid=(1,),
        in_specs=[_const_spec(res.shape), _const_spec(a.shape), _const_spec(w.shape)],
        out_specs=_const_spec(res.shape), out_shape=jax.ShapeDtypeStruct(res.shape, F32), name="proj_res",
        compiler_params=pltpu.CompilerParams(vmem_limit_bytes=VMEM_LIMIT),
    )(res, a, w)


def _l0_sample_kernel(n_tok, u_ref, cw_ref, cb_ref, dtb_ref, alog_ref, dcol_ref, sng_ref,
                      ccw_ref, ccb_ref, lng_ref, lnb_ref, h0_ref, sh0_ref, ch0_ref,
                      yc_ref, hT_ref, sh_ref, chh_ref,
                      xbc_s, gl_s, act_s, pad_s, yT_s, conv_s):
    R = 8
    u = u_ref[0]
    xbc_s[0:SSD_HIST_PAD, :] = sh0_ref[0]
    xbc_s[SSD_HIST_PAD:SSD_HIST_PAD + R, :] = u[:, C_XBC:C_XBC + SSD_CONV_DIM]
    _ssd_conv_silu(xbc_s, cw_ref, cb_ref, R, act_s)

    dt = _softplus(u[:, C_DT:C_DT + LANES] + dtb_ref[...])
    a = dt * (-jnp.exp(alog_ref[...]))
    pad_s[...] = jnp.zeros(pad_s.shape, F32)
    pad_s[0:R, 0:LANES] = dt
    dtT = pad_s[:, 0:LANES].T
    pad_s[0:R, 0:LANES] = a
    decT = jnp.exp(pad_s[:, 0:LANES].T)
    pad_s[0:R, :] = act_s[:, 0:D_MODEL]
    xT = pad_s[...].T

    P = SSD_HEAD_DIM
    hpg = SSD_HEADS // SSD_GROUPS
    yT_s[...] = jnp.zeros(yT_s.shape, F32)
    for h in range(SSD_HEADS):
        g = h // hpg
        rows = slice(h * P, (h + 1) * P)
        hh = h0_ref[0, rows, :]
        for t in range(n_tok):
            b_t = act_s[t:t + 1, D_MODEL + g * SSD_STATE:D_MODEL + (g + 1) * SSD_STATE]
            c_t = act_s[t:t + 1, D_MODEL + (SSD_GROUPS + g) * SSD_STATE:D_MODEL + (SSD_GROUPS + g + 1) * SSD_STATE]
            xdt = xT[rows, t:t + 1] * dtT[h:h + 1, t:t + 1]
            hh = hh * decT[h:h + 1, t:t + 1] + xdt * b_t
            yT_s[rows, t:t + 1] = jnp.sum(hh * c_t, axis=-1, keepdims=True)
        hT_ref[0, rows, :] = hh
    y = (yT_s[...] + dcol_ref[...] * xT).T[0:R, :]
    _gated_group_norm(y, u[:, C_Z:C_Z + D_MODEL], sng_ref, yc_ref.at[0])

    gl_s[0:CONF_HIST_PAD, :] = ch0_ref[0]
    gl_s[CONF_HIST_PAD:CONF_HIST_PAD + R, :] = u[:, C_GA:C_GA + D_MODEL] * _sigmoid(u[:, C_GB:C_GB + D_MODEL])
    _conformer(gl_s, R, u[:, C_CG:C_CG + D_MODEL], ccw_ref, ccb_ref, lng_ref, lnb_ref, conv_s, yc_ref.at[0])

    sh_ref[0] = xbc_s[...]
    chh_ref[0] = gl_s[...]


def _l0_sample(u, p, h0, sh0, ch0, n_tok):
    B = u.shape[0]
    seq_blk = lambda *s: pl.BlockSpec((1,) + s, lambda b: (b,) + (0,) * len(s))
    in_specs = [seq_blk(8, IN0_PAD),
                _const_spec((SSD_CONV, SSD_CONV_DIM)), _const_spec((1, SSD_CONV_DIM)),
                _const_spec((1, LANES)), _const_spec((1, LANES)), _const_spec((D_MODEL, 1)), _const_spec((1, D_MODEL)),
                _const_spec((CONF_CONV, D_MODEL)), _const_spec((1, D_MODEL)), _const_spec((1, D_MODEL)),
                _const_spec((1, D_MODEL)),
                seq_blk(D_MODEL, SSD_STATE), seq_blk(SSD_HIST_PAD, SSD_CONV_DIM), seq_blk(CONF_HIST_PAD, D_MODEL)]
    out_shape = (jax.ShapeDtypeStruct((B, 8, 2 * D_MODEL), F32), jax.ShapeDtypeStruct((B, D_MODEL, SSD_STATE), F32),
                 jax.ShapeDtypeStruct((B, SSD_HIST_PAD + 8, SSD_CONV_DIM), F32),
                 jax.ShapeDtypeStruct((B, CONF_HIST_PAD + 8, D_MODEL), F32))
    out_specs = (seq_blk(8, 2 * D_MODEL), seq_blk(D_MODEL, SSD_STATE), seq_blk(SSD_HIST_PAD + 8, SSD_CONV_DIM),
                 seq_blk(CONF_HIST_PAD + 8, D_MODEL))
    scratch = [pltpu.VMEM((SSD_HIST_PAD + 8, SSD_CONV_DIM), F32), pltpu.VMEM((CONF_HIST_PAD + 8, D_MODEL), F32),
               pltpu.VMEM((8, SSD_CONV_DIM), F32), pltpu.VMEM((LANES, D_MODEL), F32),
               pltpu.VMEM((D_MODEL, LANES), F32), pltpu.VMEM((8, D_MODEL), F32)]
    return pl.pallas_call(
        functools.partial(_l0_sample_kernel, n_tok), grid=(B,), in_specs=in_specs, out_specs=out_specs,
        out_shape=out_shape, scratch_shapes=scratch, name="l0_sample",
        compiler_params=pltpu.CompilerParams(dimension_semantics=("arbitrary",), vmem_limit_bytes=VMEM_LIMIT),
    )(u, p["cw"], p["cb"], p["dtb"], p["alog"], p["dcol"], p["sng"], p["ccw"], p["ccb"], p["lng"], p["lnb"],
      h0, sh0, ch0)


def _prep_l0_params(norm_g, w_in, conv_w, conv_b, dt_bias, a_log, d_skip, ssd_norm_g,
                    cconv_w, cconv_b, cln_g, cln_b, w_out):
    o_dt = D_MODEL + SSD_CONV_DIM
    w = jnp.concatenate([w_in[:, :o_dt], w_in[:, o_dt + SSD_HEADS:], w_in[:, o_dt:o_dt + SSD_HEADS],
                         jnp.zeros((D_MODEL, LANES - SSD_HEADS), w_in.dtype)], axis=1)
    lane_pad = lambda v: jnp.pad(v.astype(F32), (0, LANES - SSD_HEADS)).reshape(1, LANES)
    return dict(ng=norm_g.reshape(1, -1), w_in=w.astype(BF16), cw=conv_w, cb=conv_b.reshape(1, -1),
                dtb=lane_pad(dt_bias), alog=lane_pad(a_log),
                dcol=jnp.repeat(d_skip.astype(F32), SSD_HEAD_DIM).reshape(-1, 1), sng=ssd_norm_g.reshape(1, -1),
                ccw=cconv_w, ccb=cconv_b.reshape(1, -1), lng=cln_g.reshape(1, -1), lnb=cln_b.reshape(1, -1),
                w_out=w_out.astype(BF16))


def _hybrid_layer(x_p, x_s, state_ssm, ssd_hist, conf_hist, p):
    S = x_p.shape[1]
    B, T, _ = x_s.shape
    H, P, N = SSD_HEADS, SSD_HEAD_DIM, SSD_STATE
    zeros = lambda *s: jnp.zeros(s, F32)
    yp, h_p, sh_p, ch_p = _l0_prompt(x_p.reshape(S, D_MODEL), p, zeros(D_MODEL, N), zeros(SSD_HIST_PAD, SSD_CONV_DIM),
                                     zeros(CONF_HIST_PAD, D_MODEL))
    xs2 = x_s.reshape(B * T, D_MODEL)
    u_s = _norm_proj(xs2, p["ng"], p["w_in"]).reshape(B, T, IN0_PAD)
    u_s = jnp.pad(u_s, ((0, 0), (0, 8 - T), (0, 0)))
    sh0 = jnp.pad(ssd_hist, ((0, 0), (SSD_HIST_PAD - (SSD_CONV - 1), 0), (0, 0)))
    ch0 = jnp.pad(conf_hist, ((0, 0), (CONF_HIST_PAD - (CONF_CONV - 1), 0), (0, 0)))
    yc_s, h_s, sh_s, ch_s = _l0_sample(u_s, p, state_ssm.reshape(B, H * P, N), sh0, ch0, T)
    ys = _proj_res(xs2, yc_s[:, :T].reshape(B * T, 2 * D_MODEL), p["w_out"])
    return (yp.reshape(1, S, D_MODEL), ys.reshape(B, T, D_MODEL),
            h_p.reshape(1, H, P, N), h_s.reshape(B, H, P, N),
            sh_p[None, SSD_HIST_PAD - (SSD_CONV - 1):], sh_s[:, SSD_HIST_PAD + T - (SSD_CONV - 1):SSD_HIST_PAD + T],
            ch_p[None, CONF_HIST_PAD - (CONF_CONV - 1):], ch_s[:, CONF_HIST_PAD + T - (CONF_CONV - 1):CONF_HIST_PAD + T])


ATT_WIDTH = ATT_HEADS * HEAD_DIM
R_Q, R_K, R_V, R_G, IN1_ROWS = 0, ATT_WIDTH, ATT_WIDTH + KV_WIDTH, ATT_WIDTH + 2 * KV_WIDTH, 2 * ATT_WIDTH + 2 * KV_WIDTH
SCALE = HEAD_DIM ** -0.5
LOG2_E = 1.4426950408889634


def _head_rms_cols(xT, g_col):
    out = []
    for h in range(xT.shape[0] // HEAD_DIM):
        xh = xT[h * HEAD_DIM:(h + 1) * HEAD_DIM, :]
        r = lax.rsqrt(jnp.mean(xh * xh, axis=0, keepdims=True) + NORM_EPS)
        out.append(xh * r * g_col)
    return out


def _l1_proj_kernel(x_ref, ng_ref, wT_ref, qg_ref, kg_ref, qT_ref, gT_ref, kn_ref, vn_ref, kb_ref, vT_ref, km_ref):
    xn = _rms_rows(x_ref[...], ng_ref[...]).astype(BF16)
    uT = lax.dot_general(wT_ref[...], xn, _NT, preferred_element_type=F32)
    for h, qh in enumerate(_head_rms_cols(uT[R_Q:R_K, :], qg_ref[...])):
        qT_ref[h * HEAD_DIM:(h + 1) * HEAD_DIM, :] = (qh * (SCALE * LOG2_E)).astype(BF16)
    kT = jnp.concatenate(_head_rms_cols(uT[R_K:R_V, :], kg_ref[...]), axis=0)
    kn_ref[...] = kT
    k_nat = kT.T
    kb_ref[0] = k_nat.astype(BF16)
    km_ref[0] = jnp.mean(k_nat, axis=0, keepdims=True)
    vT = uT[R_V:R_G, :]
    vn_ref[...] = vT
    vT_ref[0] = vT.astype(BF16)
    gT_ref[...] = _silu(uT[R_G:IN1_ROWS, :])


def _l1_proj(x, ng, wT, qg_col, kg_col):
    S = x.shape[0]
    nb = S // CHUNK
    col_blk = lambda r: pl.BlockSpec((r, CHUNK), lambda i: (0, i))
    row_blk = lambda w: pl.BlockSpec((CHUNK, w), lambda i: (i, 0))
    blk3 = lambda a, b: pl.BlockSpec((1, a, b), lambda i: (i, 0, 0))
    out_shape = (jax.ShapeDtypeStruct((ATT_WIDTH, S), BF16), jax.ShapeDtypeStruct((ATT_WIDTH, S), F32),
                 jax.ShapeDtypeStruct((KV_WIDTH, S), F32), jax.ShapeDtypeStruct((KV_WIDTH, S), F32),
                 jax.ShapeDtypeStruct((nb, CHUNK, KV_WIDTH), BF16), jax.ShapeDtypeStruct((nb, KV_WIDTH, CHUNK), BF16),
                 jax.ShapeDtypeStruct((nb, 1, KV_WIDTH), F32))
    out_specs = (col_blk(ATT_WIDTH), col_blk(ATT_WIDTH), col_blk(KV_WIDTH), col_blk(KV_WIDTH),
                 blk3(CHUNK, KV_WIDTH), blk3(KV_WIDTH, CHUNK), blk3(1, KV_WIDTH))
    return pl.pallas_call(
        _l1_proj_kernel, grid=(nb,),
        in_specs=[row_blk(D_MODEL), _const_spec((1, D_MODEL)), _const_spec((IN1_ROWS, D_MODEL)),
                  _const_spec((HEAD_DIM, 1)), _const_spec((HEAD_DIM, 1))],
        out_specs=out_specs, out_shape=out_shape, name="l1_proj",
        compiler_params=pltpu.CompilerParams(dimension_semantics=("arbitrary",), vmem_limit_bytes=VMEM_LIMIT),
    )(x, ng, wT, qg_col, kg_col)


def _top_k_bias(gate, idx, n_valid_mask, axis):
    n = gate.shape[axis]
    g = jnp.where(n_valid_mask, gate, -jnp.inf)
    sel = jnp.zeros(gate.shape, F32)
    for _ in range(MOBA_TOP_K):
        mx = jnp.max(g, axis=axis, keepdims=True)
        first = jnp.min(jnp.where(g == mx, idx, n), axis=axis, keepdims=True)
        pick = jnp.logical_and(idx == first, mx > -jnp.inf)
        sel = jnp.where(pick, 1.0, sel)
        g = jnp.where(pick, -jnp.inf, g)
    return jnp.where(sel > 0.0, 0.0, MASKED)


def _attn_prompt_kernel(qT_ref, gT_ref, r_ref, kb_ref, vT_ref, km_ref, wo_ref, o_ref, qz_s, bias_s, m_s, l_s, a_s,
                        acc_s, s_s, p_s):
    t = pl.program_id(0)
    L = CHUNK
    nb = km_ref.shape[0]
    gsz = ATT_HEADS // KV_HEADS
    hd = HEAD_DIM

    zeros = jnp.zeros((hd, L), BF16)
    for hq in range(ATT_HEADS):
        q = qT_ref[hq * hd:(hq + 1) * hd, :]
        qz_s[hq] = jnp.concatenate([q, zeros] if (hq // gsz) % 2 == 0 else [zeros, q], axis=0)

    n_i = lax.broadcasted_iota(jnp.int32, (nb, L), 0)
    for hq in range(ATT_HEADS):
        j = (hq // gsz) // 2
        gate = jnp.dot(km_ref[:, j * LANES:(j + 1) * LANES].astype(BF16), qz_s[hq], preferred_element_type=F32)
        bias_s[hq] = _top_k_bias(gate, n_i, n_i < t, 0)

    def scores(hq, n):
        j = (hq // gsz) // 2
        return jnp.dot(kb_ref[n, :, j * LANES:(j + 1) * LANES], qz_s[hq], preferred_element_type=F32)

    def v_rows(hq, n):
        kvh = hq // gsz
        return vT_ref[n, kvh * hd:(kvh + 1) * hd, :]

    causalT = lax.broadcasted_iota(jnp.int32, (L, L), 0) <= lax.broadcasted_iota(jnp.int32, (L, L), 1)

    def attend(n, own):
        for hq in range(ATT_HEADS):
            s = scores(hq, n)
            s_s[hq] = jnp.where(causalT, s, MASKED) if own else s + bias_s[hq, pl.ds(n, 1), :]
        for hq in range(ATT_HEADS):
            m_new = jnp.max(s_s[hq], axis=0, keepdims=True)
            if not own:
                m_old = m_s[hq]
                m_new = jnp.maximum(m_old, m_new)
                alpha = jnp.exp2(m_old - m_new)
                a_s[hq] = alpha
            p = jnp.exp2(s_s[hq] - m_new)
            p_s[hq] = p.astype(BF16)
            psum = jnp.sum(p, axis=0, keepdims=True)
            l_s[hq] = psum if own else alpha * l_s[hq] + psum
            m_s[hq] = m_new
        for hq in range(ATT_HEADS):
            rows = slice(hq * hd, (hq + 1) * hd)
            pv = jnp.dot(v_rows(hq, n), p_s[hq], preferred_element_type=F32)
            acc_s[rows, :] = pv if own else a_s[hq] * acc_s[rows, :] + pv

    attend(t, True)

    def past_block(n, carry):
        attend(n, False)
        return carry

    lax.fori_loop(0, t, past_block, 0)

    for hq in range(ATT_HEADS):
        rows = slice(hq * hd, (hq + 1) * hd)
        acc_s[rows, :] = acc_s[rows, :] / l_s[hq] * gT_ref[rows, :]
    og = acc_s[...].T.astype(BF16)
    o_ref[...] = r_ref[...] + jnp.dot(og, wo_ref[...], preferred_element_type=F32)


def _attn_prompt(qT, gT, res, kb, vT, km, wo):
    S = res.shape[0]
    nb = S // CHUNK
    col_blk = pl.BlockSpec((ATT_WIDTH, CHUNK), lambda i: (0, i))
    row_blk = pl.BlockSpec((CHUNK, D_MODEL), lambda i: (i, 0))
    scratch = [pltpu.VMEM((ATT_HEADS, 2 * HEAD_DIM, CHUNK), BF16), pltpu.VMEM((ATT_HEADS, nb, CHUNK), F32),
               pltpu.VMEM((ATT_HEADS, 1, CHUNK), F32), pltpu.VMEM((ATT_HEADS, 1, CHUNK), F32),
               pltpu.VMEM((ATT_HEADS, 1, CHUNK), F32), pltpu.VMEM((ATT_WIDTH, CHUNK), F32),
               pltpu.VMEM((ATT_HEADS, CHUNK, CHUNK), F32), pltpu.VMEM((ATT_HEADS, CHUNK, CHUNK), BF16)]
    return pl.pallas_call(
        _attn_prompt_kernel, grid=(nb,),
        in_specs=[col_blk, col_blk, row_blk, _const_spec(kb.shape), _const_spec(vT.shape), _const_spec(km.shape),
                  _const_spec(wo.shape)],
        out_specs=row_blk, out_shape=jax.ShapeDtypeStruct((S, D_MODEL), F32), scratch_shapes=scratch,
        name="attn_prompt",
        compiler_params=pltpu.CompilerParams(dimension_semantics=("arbitrary",), vmem_limit_bytes=VMEM_LIMIT),
    )(qT, gT, res, kb, vT, km, wo)


def _pair_rms(x, g2):
    lo = lax.broadcasted_iota(jnp.int32, (x.shape[0], LANES), 1) < HEAD_DIM
    out = []
    for c in range(0, x.shape[1], LANES):
        xt = x[:, c:c + LANES]
        sq = xt * xt
        s_lo = jnp.sum(jnp.where(lo, sq, 0.0), axis=-1, keepdims=True)
        s_hi = jnp.sum(jnp.where(lo, 0.0, sq), axis=-1, keepdims=True)
        r = jnp.where(lo, lax.rsqrt(s_lo / HEAD_DIM + NORM_EPS), lax.rsqrt(s_hi / HEAD_DIM + NORM_EPS))
        out.append(xt * r * g2)
    return jnp.concatenate(out, axis=-1)


def _l1_proj_nat_kernel(x_ref, ng_ref, w_ref, qg_ref, kg_ref, q_ref, k_ref, v_ref, g_ref):
    xn = _rms_rows(x_ref[...], ng_ref[...]).astype(BF16)
    u = jnp.dot(xn, w_ref[...], preferred_element_type=F32)
    q_ref[...] = _pair_rms(u[:, R_Q:R_K], qg_ref[...]) * SCALE
    k_ref[...] = _pair_rms(u[:, R_K:R_V], kg_ref[...])
    v_ref[...] = u[:, R_V:R_G]
    g_ref[...] = _silu(u[:, R_G:IN1_ROWS])


def _l1_proj_nat(x, ng, w, qg2, kg2):
    m = x.shape[0]
    shapes = ((m, ATT_WIDTH), (m, KV_WIDTH), (m, KV_WIDTH), (m, ATT_WIDTH))
    return pl.pallas_call(
        _l1_proj_nat_kernel, grid=(1,),
        in_specs=[_const_spec(x.shape), _const_spec(ng.shape), _const_spec(w.shape), _const_spec(qg2.shape),
                  _const_spec(kg2.shape)],
        out_specs=tuple(_const_spec(s) for s in shapes),
        out_shape=tuple(jax.ShapeDtypeStruct(s, F32) for s in shapes), name="l1_proj_nat",
        compiler_params=pltpu.CompilerParams(vmem_limit_bytes=VMEM_LIMIT),
    )(x, ng, w, qg2, kg2)


def _attn_sample_kernel(n_tok, bps, pt_ref, q_ref, kn_ref, vn_ref, *refs):
    ppb = CHUNK // PAGE
    npg = bps * ppb
    k_refs, v_refs, o_ref = refs[:npg], refs[npg:2 * npg], refs[2 * npg]
    qh_s, new_s, s_s, m_s, l_s, g_s, o_s, oh_s = refs[2 * npg + 1:]
    step = pl.program_id(1)
    NB = m_s.shape[0] - 1
    R = 8
    gsz = ATT_HEADS // KV_HEADS
    GR = gsz * R
    hd = HEAD_DIM

    @pl.when(step == 0)
    def _():
        q = q_ref[0]
        for hq in range(ATT_HEADS):
            kvh, g = divmod(hq, gsz)
            qh_s[kvh, g * R:(g + 1) * R, :] = q[:, hq * hd:(hq + 1) * hd]

    def partial_softmax(kvh, s, v_h, slot, v_is_transposed):
        m = jnp.max(s, axis=-1, keepdims=True)
        p = jnp.exp(s - m)
        m_s[slot, kvh] = jnp.broadcast_to(m, (GR, hd))
        l_s[slot, kvh] = jnp.broadcast_to(jnp.sum(p, axis=-1, keepdims=True), (GR, hd))
        pv_dims = _NT if v_is_transposed else (((1,), (0,)), ((), ()))
        o_s[slot, kvh] = lax.dot_general(p.astype(BF16), v_h, pv_dims, preferred_element_type=F32)

    def head_T(page_refs, pages, kvh):
        return jnp.concatenate([page_refs[j][0, kvh] for j in pages], axis=-1).astype(BF16)

    for kvh in range(KV_HEADS):
        s_s[kvh] = jnp.dot(qh_s[kvh].astype(BF16), head_T(k_refs, range(npg), kvh), preferred_element_type=F32)
    for kvh in range(KV_HEADS):
        for i in range(bps):
            blk = step * bps + i
            s = s_s[kvh, :, i * CHUNK:(i + 1) * CHUNK]
            g_s[blk, kvh] = jnp.broadcast_to(jnp.sum(s, axis=-1, keepdims=True), (GR, hd))
            partial_softmax(kvh, s, head_T(v_refs, range(i * ppb, (i + 1) * ppb), kvh), blk, True)

    @pl.when(step == pl.num_programs(1) - 1)
    def _():
        new_s[...] = jnp.zeros(new_s.shape, F32)
        kn = kn_ref[0]
        vn = vn_ref[0]
        row_tok = lax.broadcasted_iota(jnp.int32, (GR, LANES), 0) % R
        key = lax.broadcasted_iota(jnp.int32, (GR, LANES), 1)
        own_ok = jnp.logical_and(key <= row_tok, key < n_tok)
        for kvh in range(KV_HEADS):
            new_s[0, kvh, 0:R, :] = kn[:, kvh * hd:(kvh + 1) * hd]
            new_s[1, kvh, 0:R, :] = vn[:, kvh * hd:(kvh + 1) * hd]
            qb = qh_s[kvh].astype(BF16)
            s = lax.dot_general(qb, new_s[0, kvh].astype(BF16), _NT, preferred_element_type=F32)
            partial_softmax(kvh, jnp.where(own_ok, s, MASKED), new_s[1, kvh].astype(BF16), NB, False)

            v1 = v2 = v3 = jnp.full((GR, hd), -jnp.inf, F32)
            i1 = i2 = i3 = jnp.full((GR, hd), -1.0, F32)
            for b in range(NB):
                g = g_s[b, kvh]
                c1, c2, c3 = g > v1, g > v2, g > v3
                v3, i3 = jnp.where(c2, v2, jnp.where(c3, g, v3)), jnp.where(c2, i2, jnp.where(c3, float(b), i3))
                v2, i2 = jnp.where(c1, v1, jnp.where(c2, g, v2)), jnp.where(c1, i1, jnp.where(c2, float(b), i2))
                v1, i1 = jnp.where(c1, g, v1), jnp.where(c1, float(b), i1)
            picked = lambda b: jnp.logical_or(jnp.logical_or(i1 == float(b), i2 == float(b)), i3 == float(b))
            m_tot = m_s[NB, kvh]
            for b in range(NB):
                m_tot = jnp.maximum(m_tot, jnp.where(picked(b), m_s[b, kvh], MASKED))
            w = jnp.exp(m_s[NB, kvh] - m_tot)
            num = w * o_s[NB, kvh]
            den = w * l_s[NB, kvh]
            for b in range(NB):
                w = jnp.where(picked(b), jnp.exp(m_s[b, kvh] - m_tot), 0.0)
                num = num + w * o_s[b, kvh]
                den = den + w * l_s[b, kvh]
            oh_s[kvh] = num / den
        for tile in range(ATT_HEADS // 2):
            pieces = []
            for hq in (2 * tile, 2 * tile + 1):
                kvh, g = divmod(hq, gsz)
                pieces.append(oh_s[kvh, g * R:(g + 1) * R, :])
            o_ref[0, :, tile * LANES:(tile + 1) * LANES] = jnp.concatenate(pieces, axis=-1)


def _attn_sample(page_table, q8, kn8, vn8, cache_kT, cache_vT, n_tok):
    B, n_pages = page_table.shape
    ppb = CHUNK // PAGE
    assert n_pages % ppb == 0, "the new tokens' block is assumed to hold no cached keys"
    nblk = n_pages // ppb
    bps = SAMPLE_BLOCKS_PER_STEP if nblk % SAMPLE_BLOCKS_PER_STEP == 0 else 1
    npg = bps * ppb
    GR = ATT_HEADS // KV_HEADS * 8
    seq_blk = lambda w: pl.BlockSpec((1, 8, w), lambda b, s, pt: (b, 0, 0))
    page_blk = lambda i: pl.BlockSpec((1, KV_HEADS, HEAD_DIM, PAGE), lambda b, s, pt: (pt[b, npg * s + i], 0, 0, 0))
    pages = [page_blk(i) for i in range(npg)]
    part = lambda: pltpu.VMEM((nblk + 1, KV_HEADS, GR, HEAD_DIM), F32)
    grid_spec = pltpu.PrefetchScalarGridSpec(
        num_scalar_prefetch=1, grid=(B, nblk // bps),
        in_specs=[seq_blk(ATT_WIDTH), seq_blk(KV_WIDTH), seq_blk(KV_WIDTH)] + pages + pages,
        out_specs=seq_blk(ATT_WIDTH),
        scratch_shapes=[pltpu.VMEM((KV_HEADS, GR, HEAD_DIM), F32), pltpu.VMEM((2, KV_HEADS, LANES, HEAD_DIM), F32),
                        pltpu.VMEM((KV_HEADS, GR, bps * CHUNK), F32), part(), part(), part(), part(),
                        pltpu.VMEM((KV_HEADS, GR, HEAD_DIM), F32)])
    return pl.pallas_call(
        functools.partial(_attn_sample_kernel, n_tok, bps), grid_spec=grid_spec,
        out_shape=jax.ShapeDtypeStruct((B, 8, ATT_WIDTH), F32), name="attn_sample",
        compiler_params=pltpu.CompilerParams(dimension_semantics=("arbitrary", "arbitrary"), vmem_limit_bytes=VMEM_LIMIT),
    )(page_table, q8, kn8, vn8, *([cache_kT] * npg), *([cache_vT] * npg))


def _gated_proj_res_kernel(r_ref, a_ref, g_ref, w_ref, o_ref):
    og = (a_ref[...] * g_ref[...]).astype(BF16)
    o_ref[...] = r_ref[...] + jnp.dot(og, w_ref[...], preferred_element_type=F32)


def _gated_proj_res(res, a, g, w):
    return pl.pallas_call(
        _gated_proj_res_kernel, grid=(1,),
        in_specs=[_const_spec(res.shape), _const_spec(a.shape), _const_spec(g.shape), _const_spec(w.shape)],
        out_specs=_const_spec(res.shape), out_shape=jax.ShapeDtypeStruct(res.shape, F32), name="gated_proj_res",
        compiler_params=pltpu.CompilerParams(vmem_limit_bytes=VMEM_LIMIT),
    )(res, a, g, w)


def _attention_layer(yp, ys, cache_k, cache_v, page_table, norm_g, w_in, qn_g, kn_g, w_out):
    S = yp.shape[1]
    B, T, _ = ys.shape
    ng = norm_g.reshape(1, -1)
    wo = w_out.astype(BF16)
    qT, gT, k_p, v_p, kb, vT, km = _l1_proj(yp.reshape(S, D_MODEL), ng, w_in.T.astype(BF16),
                                            qn_g.reshape(-1, 1), kn_g.reshape(-1, 1))
    yp2 = _attn_prompt(qT, gT, yp.reshape(S, D_MODEL), kb, vT, km.reshape(-1, KV_WIDTH), wo)

    ys2 = ys.reshape(B * T, D_MODEL)
    pair = lambda g: jnp.concatenate([g, g]).reshape(1, LANES)
    q_s, k_s, v_s, g_s = _l1_proj_nat(ys2, ng, w_in.astype(BF16), pair(qn_g), pair(kn_g))
    pad_tok = lambda a: jnp.pad(a.reshape(B, T, -1), ((0, 0), (0, 8 - T), (0, 0)))
    paged = lambda c: jnp.transpose(c, (0, 2, 3, 1))
    o_s = _attn_sample(page_table, pad_tok(q_s), pad_tok(k_s), pad_tok(v_s), paged(cache_k), paged(cache_v), T)
    ys3 = _gated_proj_res(ys2, o_s[:, :T].reshape(B * T, ATT_WIDTH), g_s, wo)
    tok_major = lambda aT: jnp.transpose(aT.reshape(KV_HEADS, HEAD_DIM, S), (2, 0, 1))[None]
    return (yp2.reshape(1, S, D_MODEL), ys3.reshape(B, T, D_MODEL), tok_major(k_p), tok_major(v_p),
            k_s.reshape(B, T, KV_HEADS, HEAD_DIM), v_s.reshape(B, T, KV_HEADS, HEAD_DIM))


def kernel(x_prompt, x_sample, state_ssm, state_ssd_conv, state_conf_conv, cache_k, cache_v, page_table, norm0_g, w_in0, ssd_conv_w, ssd_conv_b, ssd_dt_bias, ssd_a_log, ssd_d, ssd_norm_g, conf_conv_w, conf_conv_b, conf_ln_g, conf_ln_b, w_out0, norm1_g, w_in1, q_norm_g, k_norm_g, w_out1):
    p0 = _prep_l0_params(norm0_g[0], w_in0[0], ssd_conv_w[0], ssd_conv_b[0], ssd_dt_bias[0], ssd_a_log[0], ssd_d[0],
                         ssd_norm_g[0], conf_conv_w[0], conf_conv_b[0], conf_ln_g[0], conf_ln_b[0], w_out0[0])
    yp, ys, h_p, h_s, sh_p, sh_s, ch_p, ch_s = _hybrid_layer(
        x_prompt, x_sample, state_ssm[0], state_ssd_conv[0], state_conf_conv[0], p0)
    yp, ys, k_p, v_p, k_s, v_s = _attention_layer(yp, ys, cache_k[0], cache_v[0], page_table, norm1_g[0], w_in1[0],
                                                  q_norm_g[0], k_norm_g[0], w_out1[0])
    return (yp, ys, h_p[None], h_s[None], sh_p[None], sh_s[None], ch_p[None], ch_s[None],
            k_p[None], v_p[None], k_s[None], v_s[None])
```

```python
import functools

import jax
import jax.numpy as jnp
from jax import lax
from jax.experimental import pallas as pl
from jax.experimental.pallas import tpu as pltpu

F32 = jnp.float32
BF16 = jnp.bfloat16
NORM_EPS = 1e-6
MASKED = -1e30
LANES = 128
SUBLANES = 8
CHUNK = 256
D_MODEL = 1024
SSD_HEADS = 16
SSD_HEAD_DIM = 64
SSD_GROUPS = 4
SSD_STATE = 128
SSD_CONV = 4
SSD_CONV_DIM = 2048
CONF_CONV = 31
ATT_HEADS = 16
KV_HEADS = 4
HEAD_DIM = 64
KV_WIDTH = KV_HEADS * HEAD_DIM
MOBA_TOP_K = 3
PAGE = 128
C_Z, C_XBC, C_GA, C_GB, C_CG, C_DT, IN0_PAD = 0, 1024, 3072, 4096, 5120, 6144, 6272
SSD_HIST_PAD = 8
CONF_HIST_PAD = 32
SAMPLE_CHUNK = 128
SAMPLE_BLOCKS_PER_STEP = 8
VMEM_LIMIT = 56 * 1024 * 1024

_NT = (((1,), (1,)), ((), ()))


def _sigmoid(x):
    return 1.0 / (1.0 + jnp.exp(-x))


def _silu(x):
    return x * _sigmoid(x)


def _softplus(x):
    return jnp.maximum(x, 0.0) + jnp.log1p(jnp.exp(-jnp.abs(x)))


def _rms_rows(x, g):
    ms = jnp.mean(x * x, axis=-1, keepdims=True)
    return x * lax.rsqrt(ms + NORM_EPS) * g


def _const_spec(shape):
    nd = len(shape)
    return pl.BlockSpec(shape, lambda *_: (0,) * nd)


def _ssd_conv_silu(buf_ref, cw_ref, cb_ref, rows, out_ref):
    for c in range(0, SSD_CONV_DIM, 512):
        acc = cb_ref[:, c:c + 512]
        for k in range(SSD_CONV):
            o = SSD_HIST_PAD - (SSD_CONV - 1) + k
            acc = acc + cw_ref[k:k + 1, c:c + 512] * buf_ref[o:o + rows, c:c + 512]
        out_ref[:, c:c + 512] = _silu(acc)


def _gated_group_norm(y, z, sng_ref, yc_ref):
    y = y * _silu(z)
    gw = D_MODEL // SSD_GROUPS
    for g in range(SSD_GROUPS):
        yg = y[:, g * gw:(g + 1) * gw]
        r = lax.rsqrt(jnp.mean(yg * yg, axis=-1, keepdims=True) + NORM_EPS)
        yc_ref[:, g * gw:(g + 1) * gw] = (yg * r * sng_ref[:, g * gw:(g + 1) * gw]).astype(yc_ref.dtype)


CONV_WIN_EXTRA = (CONF_CONV - 1) // SUBLANES * SUBLANES


def _conv_window_scratch(rows):
    return pltpu.VMEM((SUBLANES, min(rows, 128) + CONV_WIN_EXTRA, LANES), F32)


def _conformer(gl_ref, rows, cgate, ccw_ref, ccb_ref, lng_ref, lnb_ref, conv_ref, win_ref, yc_ref):
    first = CONF_HIST_PAD - (CONF_CONV - 1)
    rb = min(rows, 128)
    for c in range(0, D_MODEL, LANES):
        for r0 in range(0, rows, rb):
            acc = jnp.broadcast_to(ccb_ref[:, c:c + LANES], (rb, LANES))
            for phase in range(SUBLANES):
                taps = [k for k in range(CONF_CONV) if (first + k) % SUBLANES == phase]
                if not taps:
                    continue
                lo, hi = first + taps[0], first + taps[-1]
                n_win = hi - lo + rb
                win_ref[phase, 0:n_win, :] = gl_ref[r0 + lo:r0 + hi + rb, c:c + LANES]
                for k in taps:
                    off = first + k - lo
                    acc = acc + ccw_ref[k:k + 1, c:c + LANES] * win_ref[phase, off:off + rb, :]
            conv_ref[r0:r0 + rb, c:c + LANES] = acc
    cv = conv_ref[...]
    mu = jnp.mean(cv, axis=-1, keepdims=True)
    xc = cv - mu
    var = jnp.mean(xc * xc, axis=-1, keepdims=True)
    cn = xc * lax.rsqrt(var + NORM_EPS) * lng_ref[...] + lnb_ref[...]
    yc_ref[:, D_MODEL:2 * D_MODEL] = (_silu(cn) * _silu(cgate)).astype(yc_ref.dtype)


def _ssd_chunk(L, act_ref, dt, alog_ref, dcol_ref, h_in_ref, h_out_ref, yT_ref):
    a = dt * (-jnp.exp(alog_ref[...]))
    r_i = lax.broadcasted_iota(jnp.int32, (L, L), 0)
    c_i = lax.broadcasted_iota(jnp.int32, (L, L), 1)
    tri = (c_i <= r_i).astype(F32)
    acs = jnp.dot(tri, a, precision=lax.Precision.HIGHEST, preferred_element_type=F32)
    acsT = acs.T
    dtT = dt.T
    last = acsT[:, L - 1:L]
    exp_acsT = jnp.exp(acsT)
    dec_endT = jnp.exp(last - acsT)
    chunk_dec = jnp.exp(last)
    causalT = r_i <= c_i

    xT = act_ref[:, 0:D_MODEL].T
    P = SSD_HEAD_DIM
    hpg = SSD_HEADS // SSD_GROUPS
    gp = hpg * P
    for g in range(SSD_GROUPS):
        b_g = act_ref[:, D_MODEL + g * SSD_STATE:D_MODEL + (g + 1) * SSD_STATE].astype(BF16)
        c_g = act_ref[:, D_MODEL + (SSD_GROUPS + g) * SSD_STATE:D_MODEL + (SSD_GROUPS + g + 1) * SSD_STATE].astype(BF16)
        cbT = lax.dot_general(b_g, c_g, _NT, preferred_element_type=F32)
        h_g = h_in_ref[g * gp:(g + 1) * gp, :]
        y_offT = lax.dot_general(h_g.astype(BF16), c_g, _NT, preferred_element_type=F32)
        st_lhs = []
        for r in range(hpg):
            h = g * hpg + r
            rows = slice(h * P, (h + 1) * P)
            xT_h = xT[rows, :]
            xdtT_h = xT_h * dtT[h:h + 1, :]
            seg = acsT[h:h + 1, :] - acs[:, h:h + 1]
            mT = (jnp.exp(jnp.where(causalT, seg, -jnp.inf)) * cbT).astype(BF16)
            y_dT = jnp.dot(xdtT_h.astype(BF16), mT, preferred_element_type=F32)
            y_oT = y_offT[r * P:(r + 1) * P, :] * exp_acsT[h:h + 1, :]
            yT_ref[rows, :] = y_dT + y_oT + dcol_ref[rows, :] * xT_h
            st_lhs.append((xdtT_h * dec_endT[h:h + 1, :]).astype(BF16))
        st = jnp.dot(jnp.concatenate(st_lhs, axis=0), b_g, preferred_element_type=F32)
        for r in range(hpg):
            h = g * hpg + r
            h_out_ref[h * P:(h + 1) * P, :] = h_g[r * P:(r + 1) * P, :] * chunk_dec[h:h + 1, :] + st[r * P:(r + 1) * P, :]


def _l0_prompt_kernel(x_ref, ng_ref, w_ref, cw_ref, cb_ref, dtb_ref, alog_ref, dcol_ref, sng_ref,
                      ccw_ref, ccb_ref, lng_ref, lnb_ref, wo_ref, h0_ref, sh0_ref, ch0_ref,
                      y_ref, hT_ref, sh_ref, chh_ref,
                      u_s, xbc_s, gl_s, h_s, act_s, yT_s, conv_s, win_s, yc_s):
    i = pl.program_id(0)
    L = CHUNK

    @pl.when(i == 0)
    def _():
        h_s[...] = h0_ref[...]
        xbc_s[0:SSD_HIST_PAD, :] = sh0_ref[...]
        gl_s[0:CONF_HIST_PAD, :] = ch0_ref[...]

    x = x_ref[...]
    xn = _rms_rows(x, ng_ref[...]).astype(BF16)
    nblk = IN0_PAD // 896
    for c in range(nblk):
        u_s[:, c * 896:(c + 1) * 896] = jnp.dot(xn, w_ref[:, c * 896:(c + 1) * 896], preferred_element_type=F32)

    xbc_s[SSD_HIST_PAD:SSD_HIST_PAD + L, :] = u_s[:, C_XBC:C_XBC + SSD_CONV_DIM]
    _ssd_conv_silu(xbc_s, cw_ref, cb_ref, L, act_s)

    dt = _softplus(u_s[:, C_DT:C_DT + LANES] + dtb_ref[...])
    _ssd_chunk(L, act_s, dt, alog_ref, dcol_ref, h_s, h_s, yT_s)
    _gated_group_norm(yT_s[...].T, u_s[:, C_Z:C_Z + D_MODEL], sng_ref, yc_s)

    gl_s[CONF_HIST_PAD:CONF_HIST_PAD + L, :] = u_s[:, C_GA:C_GA + D_MODEL] * _sigmoid(u_s[:, C_GB:C_GB + D_MODEL])
    _conformer(gl_s, L, u_s[:, C_CG:C_CG + D_MODEL], ccw_ref, ccb_ref, lng_ref, lnb_ref, conv_s, win_s, yc_s)

    y_ref[...] = x + jnp.dot(yc_s[...], wo_ref[...], preferred_element_type=F32)

    xbc_s[0:SSD_HIST_PAD, :] = xbc_s[L:L + SSD_HIST_PAD, :]
    gl_s[0:CONF_HIST_PAD, :] = gl_s[L:L + CONF_HIST_PAD, :]

    @pl.when(i == pl.num_programs(0) - 1)
    def _():
        hT_ref[...] = h_s[...]
        sh_ref[...] = xbc_s[0:SSD_HIST_PAD, :]
        chh_ref[...] = gl_s[0:CONF_HIST_PAD, :]


def _l0_prompt(x, p, h0, sh0, ch0):
    S = x.shape[0]
    assert S % CHUNK == 0
    nc = S // CHUNK
    row_blk = lambda w: pl.BlockSpec((CHUNK, w), lambda i: (i, 0))
    in_specs = [row_blk(D_MODEL), _const_spec((1, D_MODEL)), _const_spec((D_MODEL, IN0_PAD)),
                _const_spec((SSD_CONV, SSD_CONV_DIM)), _const_spec((1, SSD_CONV_DIM)),
                _const_spec((1, LANES)), _const_spec((1, LANES)), _const_spec((D_MODEL, 1)), _const_spec((1, D_MODEL)),
                _const_spec((CONF_CONV, D_MODEL)), _const_spec((1, D_MODEL)), _const_spec((1, D_MODEL)),
                _const_spec((1, D_MODEL)), _const_spec((2 * D_MODEL, D_MODEL)),
                _const_spec((D_MODEL, SSD_STATE)), _const_spec((SSD_HIST_PAD, SSD_CONV_DIM)),
                _const_spec((CONF_HIST_PAD, D_MODEL))]
    out_shape = (jax.ShapeDtypeStruct((S, D_MODEL), F32), jax.ShapeDtypeStruct((D_MODEL, SSD_STATE), F32),
                 jax.ShapeDtypeStruct((SSD_HIST_PAD, SSD_CONV_DIM), F32), jax.ShapeDtypeStruct((CONF_HIST_PAD, D_MODEL), F32))
    out_specs = (row_blk(D_MODEL), _const_spec((D_MODEL, SSD_STATE)), _const_spec((SSD_HIST_PAD, SSD_CONV_DIM)),
                 _const_spec((CONF_HIST_PAD, D_MODEL)))
    scratch = [pltpu.VMEM((CHUNK, IN0_PAD), F32), pltpu.VMEM((SSD_HIST_PAD + CHUNK, SSD_CONV_DIM), F32),
               pltpu.VMEM((CONF_HIST_PAD + CHUNK, D_MODEL), F32), pltpu.VMEM((D_MODEL, SSD_STATE), F32),
               pltpu.VMEM((CHUNK, SSD_CONV_DIM), F32), pltpu.VMEM((D_MODEL, CHUNK), F32),
               pltpu.VMEM((CHUNK, D_MODEL), F32), _conv_window_scratch(CHUNK), pltpu.VMEM((CHUNK, 2 * D_MODEL), BF16)]
    return pl.pallas_call(
        _l0_prompt_kernel, grid=(nc,), in_specs=in_specs, out_specs=out_specs, out_shape=out_shape,
        scratch_shapes=scratch, name="l0_prompt",
        compiler_params=pltpu.CompilerParams(dimension_semantics=("arbitrary",), vmem_limit_bytes=VMEM_LIMIT),
    )(x, p["ng"], p["w_in"], p["cw"], p["cb"], p["dtb"], p["alog"], p["dcol"], p["sng"],
      p["ccw"], p["ccb"], p["lng"], p["lnb"], p["w_out"], h0, sh0, ch0)


def _norm_proj_kernel(x_ref, g_ref, w_ref, o_ref):
    xn = _rms_rows(x_ref[...], g_ref[...]).astype(BF16)
    o_ref[...] = jnp.dot(xn, w_ref[...], preferred_element_type=F32)


def _norm_proj(x, g, w):
    m, n = x.shape[0], w.shape[1]
    return pl.pallas_call(
        _norm_proj_kernel, grid=(1,),
        in_specs=[_const_spec(x.shape), _const_spec(g.shape), _const_spec(w.shape)],
        out_specs=_const_spec((m, n)), out_shape=jax.ShapeDtypeStruct((m, n), F32), name="norm_proj",
        compiler_params=pltpu.CompilerParams(vmem_limit_bytes=VMEM_LIMIT),
    )(x, g, w)


def _proj_res_kernel(r_ref, a_ref, w_ref, o_ref):
    o_ref[...] = r_ref[...] + jnp.dot(a_ref[...].astype(BF16), w_ref[...], preferred_element_type=F32)


def _proj_res(res, a, w):
    return pl.pallas_call(
        _proj_res_kernel, grid=(1,),
        in_specs=[_const_spec(res.shape), _const_spec(a.shape), _const_spec(w.shape)],
        out_specs=_const_spec(res.shape), out_shape=jax.ShapeDtypeStruct(res.shape, F32), name="proj_res",
        compiler_params=pltpu.CompilerParams(vmem_limit_bytes=VMEM_LIMIT),
    )(res, a, w)


def _l0_sample_kernel(n_tok, u_ref, cw_ref, cb_ref, dtb_ref, alog_ref, dcol_ref, sng_ref,
                      ccw_ref, ccb_ref, lng_ref, lnb_ref, h0_ref, sh0_ref, ch0_ref,
                      yc_ref, hT_ref, sh_ref, chh_ref,
                      xbc_s, gl_s, act_s, yT_s, conv_s, win_s):
    R = 8
    L = SAMPLE_CHUNK
    u = u_ref[0]
    xbc_s[0:SSD_HIST_PAD, :] = sh0_ref[0]
    xbc_s[SSD_HIST_PAD:SSD_HIST_PAD + R, :] = u[:, C_XBC:C_XBC + SSD_CONV_DIM]
    act_s[R:L, :] = jnp.zeros((L - R, SSD_CONV_DIM), F32)
    _ssd_conv_silu(xbc_s, cw_ref, cb_ref, R, act_s.at[0:R])

    dt = _softplus(u[:, C_DT:C_DT + LANES] + dtb_ref[...])
    dt = jnp.where(lax.broadcasted_iota(jnp.int32, (R, LANES), 0) < n_tok, dt, 0.0)
    dt = jnp.concatenate([dt, jnp.zeros((L - R, LANES), F32)], axis=0)
    _ssd_chunk(L, act_s, dt, alog_ref, dcol_ref, h0_ref.at[0], hT_ref.at[0], yT_s)
    y = yT_s[...].T[0:R, :]
    _gated_group_norm(y, u[:, C_Z:C_Z + D_MODEL], sng_ref, yc_ref.at[0])

    gl_s[0:CONF_HIST_PAD, :] = ch0_ref[0]
    gl_s[CONF_HIST_PAD:CONF_HIST_PAD + R, :] = u[:, C_GA:C_GA + D_MODEL] * _sigmoid(u[:, C_GB:C_GB + D_MODEL])
    _conformer(gl_s, R, u[:, C_CG:C_CG + D_MODEL], ccw_ref, ccb_ref, lng_ref, lnb_ref, conv_s, win_s, yc_ref.at[0])

    sh_ref[0] = xbc_s[...]
    chh_ref[0] = gl_s[...]


def _l0_sample(u, p, h0, sh0, ch0, n_tok):
    B = u.shape[0]
    seq_blk = lambda *s: pl.BlockSpec((1,) + s, lambda b: (b,) + (0,) * len(s))
    in_specs = [seq_blk(8, IN0_PAD),
                _const_spec((SSD_CONV, SSD_CONV_DIM)), _const_spec((1, SSD_CONV_DIM)),
                _const_spec((1, LANES)), _const_spec((1, LANES)), _const_spec((D_MODEL, 1)), _const_spec((1, D_MODEL)),
                _const_spec((CONF_CONV, D_MODEL)), _const_spec((1, D_MODEL)), _const_spec((1, D_MODEL)),
                _const_spec((1, D_MODEL)),
                seq_blk(D_MODEL, SSD_STATE), seq_blk(SSD_HIST_PAD, SSD_CONV_DIM), seq_blk(CONF_HIST_PAD, D_MODEL)]
    out_shape = (jax.ShapeDtypeStruct((B, 8, 2 * D_MODEL), F32), jax.ShapeDtypeStruct((B, D_MODEL, SSD_STATE), F32),
                 jax.ShapeDtypeStruct((B, SSD_HIST_PAD + 8, SSD_CONV_DIM), F32),
                 jax.ShapeDtypeStruct((B, CONF_HIST_PAD + 8, D_MODEL), F32))
    out_specs = (seq_blk(8, 2 * D_MODEL), seq_blk(D_MODEL, SSD_STATE), seq_blk(SSD_HIST_PAD + 8, SSD_CONV_DIM),
                 seq_blk(CONF_HIST_PAD + 8, D_MODEL))
    scratch = [pltpu.VMEM((SSD_HIST_PAD + 8, SSD_CONV_DIM), F32), pltpu.VMEM((CONF_HIST_PAD + 8, D_MODEL), F32),
               pltpu.VMEM((SAMPLE_CHUNK, SSD_CONV_DIM), F32),
               pltpu.VMEM((D_MODEL, SAMPLE_CHUNK), F32), pltpu.VMEM((8, D_MODEL), F32), _conv_window_scratch(8)]
    return pl.pallas_call(
        functools.partial(_l0_sample_kernel, n_tok), grid=(B,), in_specs=in_specs, out_specs=out_specs,
        out_shape=out_shape, scratch_shapes=scratch, name="l0_sample",
        compiler_params=pltpu.CompilerParams(dimension_semantics=("arbitrary",), vmem_limit_bytes=VMEM_LIMIT),
    )(u, p["cw"], p["cb"], p["dtb"], p["alog"], p["dcol"], p["sng"], p["ccw"], p["ccb"], p["lng"], p["lnb"],
      h0, sh0, ch0)


def _prep_l0_params(norm_g, w_in, conv_w, conv_b, dt_bias, a_log, d_skip, ssd_norm_g,
                    cconv_w, cconv_b, cln_g, cln_b, w_out):
    o_dt = D_MODEL + SSD_CONV_DIM
    w = jnp.concatenate([w_in[:, :o_dt], w_in[:, o_dt + SSD_HEADS:], w_in[:, o_dt:o_dt + SSD_HEADS],
                         jnp.zeros((D_MODEL, LANES - SSD_HEADS), w_in.dtype)], axis=1)
    lane_pad = lambda v: jnp.pad(v.astype(F32), (0, LANES - SSD_HEADS)).reshape(1, LANES)
    return dict(ng=norm_g.reshape(1, -1), w_in=w.astype(BF16), cw=conv_w, cb=conv_b.reshape(1, -1),
                dtb=lane_pad(dt_bias), alog=lane_pad(a_log),
                dcol=jnp.repeat(d_skip.astype(F32), SSD_HEAD_DIM).reshape(-1, 1), sng=ssd_norm_g.reshape(1, -1),
                ccw=cconv_w, ccb=cconv_b.reshape(1, -1), lng=cln_g.reshape(1, -1), lnb=cln_b.reshape(1, -1),
                w_out=w_out.astype(BF16))


def _hybrid_layer(x_p, x_s, state_ssm, ssd_hist, conf_hist, p):
    S = x_p.shape[1]
    B, T, _ = x_s.shape
    H, P, N = SSD_HEADS, SSD_HEAD_DIM, SSD_STATE
    zeros = lambda *s: jnp.zeros(s, F32)
    yp, h_p, sh_p, ch_p = _l0_prompt(x_p.reshape(S, D_MODEL), p, zeros(D_MODEL, N), zeros(SSD_HIST_PAD, SSD_CONV_DIM),
                                     zeros(CONF_HIST_PAD, D_MODEL))
    xs2 = x_s.reshape(B * T, D_MODEL)
    u_s = _norm_proj(xs2, p["ng"], p["w_in"]).reshape(B, T, IN0_PAD)
    u_s = jnp.pad(u_s, ((0, 0), (0, 8 - T), (0, 0)))
    sh0 = jnp.pad(ssd_hist, ((0, 0), (SSD_HIST_PAD - (SSD_CONV - 1), 0), (0, 0)))
    ch0 = jnp.pad(conf_hist, ((0, 0), (CONF_HIST_PAD - (CONF_CONV - 1), 0), (0, 0)))
    yc_s, h_s, sh_s, ch_s = _l0_sample(u_s, p, state_ssm.reshape(B, H * P, N), sh0, ch0, T)
    ys = _proj_res(xs2, yc_s[:, :T].reshape(B * T, 2 * D_MODEL), p["w_out"])
    return (yp.reshape(1, S, D_MODEL), ys.reshape(B, T, D_MODEL),
            h_p.reshape(1, H, P, N), h_s.reshape(B, H, P, N),
            sh_p[None, SSD_HIST_PAD - (SSD_CONV - 1):], sh_s[:, SSD_HIST_PAD + T - (SSD_CONV - 1):SSD_HIST_PAD + T],
            ch_p[None, CONF_HIST_PAD - (CONF_CONV - 1):], ch_s[:, CONF_HIST_PAD + T - (CONF_CONV - 1):CONF_HIST_PAD + T])


ATT_WIDTH = ATT_HEADS * HEAD_DIM
R_Q, R_K, R_V, R_G, IN1_ROWS = 0, ATT_WIDTH, ATT_WIDTH + KV_WIDTH, ATT_WIDTH + 2 * KV_WIDTH, 2 * ATT_WIDTH + 2 * KV_WIDTH
SCALE = HEAD_DIM ** -0.5
LOG2_E = 1.4426950408889634
V_EXT = HEAD_DIM + 16


def _head_rms_cols(xT, g_col):
    out = []
    for h in range(xT.shape[0] // HEAD_DIM):
        xh = xT[h * HEAD_DIM:(h + 1) * HEAD_DIM, :]
        r = lax.rsqrt(jnp.mean(xh * xh, axis=0, keepdims=True) + NORM_EPS)
        out.append(xh * r * g_col)
    return out


def _l1_proj_kernel(x_ref, ng_ref, wT_ref, qg_ref, kg_ref, qT_ref, gT_ref, kn_ref, vn_ref, kb_ref, vT_ref, km_ref):
    xn = _rms_rows(x_ref[...], ng_ref[...]).astype(BF16)
    uT = lax.dot_general(wT_ref[...], xn, _NT, preferred_element_type=F32)
    for h, qh in enumerate(_head_rms_cols(uT[R_Q:R_K, :], qg_ref[...])):
        qT_ref[h * HEAD_DIM:(h + 1) * HEAD_DIM, :] = (qh * (SCALE * LOG2_E)).astype(BF16)
    kT = jnp.concatenate(_head_rms_cols(uT[R_K:R_V, :], kg_ref[...]), axis=0)
    kn_ref[...] = kT
    k_nat = kT.T
    kb_ref[0] = k_nat.astype(BF16)
    km_ref[0] = jnp.mean(k_nat, axis=0, keepdims=True)
    vT = uT[R_V:R_G, :]
    vn_ref[...] = vT
    for h in range(KV_HEADS):
        vT_ref[0, h * V_EXT:h * V_EXT + HEAD_DIM, :] = vT[h * HEAD_DIM:(h + 1) * HEAD_DIM, :].astype(BF16)
        vT_ref[0, h * V_EXT + HEAD_DIM:(h + 1) * V_EXT, :] = jnp.ones((V_EXT - HEAD_DIM, vT.shape[1]), BF16)
    gT_ref[...] = _silu(uT[R_G:IN1_ROWS, :])


def _l1_proj(x, ng, wT, qg_col, kg_col):
    S = x.shape[0]
    nb = S // CHUNK
    col_blk = lambda r: pl.BlockSpec((r, CHUNK), lambda i: (0, i))
    row_blk = lambda w: pl.BlockSpec((CHUNK, w), lambda i: (i, 0))
    blk3 = lambda a, b: pl.BlockSpec((1, a, b), lambda i: (i, 0, 0))
    out_shape = (jax.ShapeDtypeStruct((ATT_WIDTH, S), BF16), jax.ShapeDtypeStruct((ATT_WIDTH, S), F32),
                 jax.ShapeDtypeStruct((KV_WIDTH, S), F32), jax.ShapeDtypeStruct((KV_WIDTH, S), F32),
                 jax.ShapeDtypeStruct((nb, CHUNK, KV_WIDTH), BF16), jax.ShapeDtypeStruct((nb, KV_HEADS * V_EXT, CHUNK), BF16),
                 jax.ShapeDtypeStruct((nb, 1, KV_WIDTH), F32))
    out_specs = (col_blk(ATT_WIDTH), col_blk(ATT_WIDTH), col_blk(KV_WIDTH), col_blk(KV_WIDTH),
                 blk3(CHUNK, KV_WIDTH), blk3(KV_HEADS * V_EXT, CHUNK), blk3(1, KV_WIDTH))
    return pl.pallas_call(
        _l1_proj_kernel, grid=(nb,),
        in_specs=[row_blk(D_MODEL), _const_spec((1, D_MODEL)), _const_spec((IN1_ROWS, D_MODEL)),
                  _const_spec((HEAD_DIM, 1)), _const_spec((HEAD_DIM, 1))],
        out_specs=out_specs, out_shape=out_shape, name="l1_proj",
        compiler_params=pltpu.CompilerParams(dimension_semantics=("arbitrary",), vmem_limit_bytes=VMEM_LIMIT),
    )(x, ng, wT, qg_col, kg_col)


def _top_k_bias(gate, idx, n_valid_mask, axis):
    n = gate.shape[axis]
    g = jnp.where(n_valid_mask, gate, -jnp.inf)
    sel = jnp.zeros(gate.shape, F32)
    for _ in range(MOBA_TOP_K):
        mx = jnp.max(g, axis=axis, keepdims=True)
        first = jnp.min(jnp.where(g == mx, idx, n), axis=axis, keepdims=True)
        pick = jnp.logical_and(idx == first, mx > -jnp.inf)
        sel = jnp.where(pick, 1.0, sel)
        g = jnp.where(pick, -jnp.inf, g)
    return jnp.where(sel > 0.0, 0.0, MASKED)


def _attn_prompt_kernel(qT_ref, gT_ref, r_ref, kb_ref, vT_ref, km_ref, wo_ref, o_ref, qz_s, bias_s, m_s, a_s,
                        acc_s, og_s, s_s, p_s):
    t = pl.program_id(0)
    L = CHUNK
    nb = km_ref.shape[0]
    gsz = ATT_HEADS // KV_HEADS
    hd = HEAD_DIM

    zeros = jnp.zeros((hd, L), BF16)
    for hq in range(ATT_HEADS):
        q = qT_ref[hq * hd:(hq + 1) * hd, :]
        qz_s[hq] = jnp.concatenate([q, zeros] if (hq // gsz) % 2 == 0 else [zeros, q], axis=0)

    n_i = lax.broadcasted_iota(jnp.int32, (nb, L), 0)
    for hq in range(ATT_HEADS):
        j = (hq // gsz) // 2
        gate = jnp.dot(km_ref[:, j * LANES:(j + 1) * LANES].astype(BF16), qz_s[hq], preferred_element_type=F32)
        bias_s[hq] = _top_k_bias(gate, n_i, n_i < t, 0)

    def scores(hq, n):
        j = (hq // gsz) // 2
        return jnp.dot(kb_ref[n, :, j * LANES:(j + 1) * LANES], qz_s[hq], preferred_element_type=F32)

    def v_rows(hq, n):
        kvh = hq // gsz
        return vT_ref[n, kvh * V_EXT:(kvh + 1) * V_EXT, :]

    causalT = lax.broadcasted_iota(jnp.int32, (L, L), 0) <= lax.broadcasted_iota(jnp.int32, (L, L), 1)

    def stage_scores(n, slot, own):
        for hq in range(ATT_HEADS):
            s = scores(hq, n).astype(BF16)
            if own:
                s_s[slot, hq] = jnp.where(causalT, s, MASKED)
            else:
                s_s[slot, hq] = s + bias_s[hq, pl.ds(n, 1), :].astype(BF16)

    def softmax_and_pv(n, slot, own):
        for hq in range(ATT_HEADS):
            m_new = jnp.max(s_s[slot, hq], axis=0, keepdims=True).astype(F32)
            if not own:
                m_old = m_s[hq]
                m_new = jnp.maximum(m_old, m_new)
                a_s[hq] = jnp.exp2(m_old - m_new)
            p_s[hq] = jnp.exp2(s_s[slot, hq] - m_new.astype(BF16))
            m_s[hq] = m_new
        for hq in range(ATT_HEADS):
            rows = slice(hq * V_EXT, (hq + 1) * V_EXT)
            pv = jnp.dot(v_rows(hq, n), p_s[hq], preferred_element_type=F32)
            acc_s[rows, :] = pv if own else a_s[hq] * acc_s[rows, :] + pv

    stage_scores(t, 0, True)
    softmax_and_pv(t, 0, True)

    def past_block(n, carry):
        stage_scores(n, 0, False)
        softmax_and_pv(n, 0, False)
        return carry

    lax.fori_loop(0, t, past_block, 0)

    for hq in range(ATT_HEADS):
        rows = slice(hq * hd, (hq + 1) * hd)
        og_s[rows, :] = acc_s[hq * V_EXT:hq * V_EXT + hd, :] / acc_s[hq * V_EXT + hd:hq * V_EXT + hd + 1, :] * gT_ref[rows, :]
    og = og_s[...].T.astype(BF16)
    o_ref[...] = r_ref[...] + jnp.dot(og, wo_ref[...], preferred_element_type=F32)


def _attn_prompt(qT, gT, res, kb, vT, km, wo):
    S = res.shape[0]
    nb = S // CHUNK
    col_blk = pl.BlockSpec((ATT_WIDTH, CHUNK), lambda i: (0, i))
    row_blk = pl.BlockSpec((CHUNK, D_MODEL), lambda i: (i, 0))
    scratch = [pltpu.VMEM((ATT_HEADS, 2 * HEAD_DIM, CHUNK), BF16), pltpu.VMEM((ATT_HEADS, nb, CHUNK), F32),
               pltpu.VMEM((ATT_HEADS, 1, CHUNK), F32), pltpu.VMEM((ATT_HEADS, 1, CHUNK), F32),
               pltpu.VMEM((ATT_HEADS * V_EXT, CHUNK), F32), pltpu.VMEM((ATT_WIDTH, CHUNK), F32),
               pltpu.VMEM((1, ATT_HEADS, CHUNK, CHUNK), BF16), pltpu.VMEM((ATT_HEADS, CHUNK, CHUNK), BF16)]
    return pl.pallas_call(
        _attn_prompt_kernel, grid=(nb,),
        in_specs=[col_blk, col_blk, row_blk, _const_spec(kb.shape), _const_spec(vT.shape), _const_spec(km.shape),
                  _const_spec(wo.shape)],
        out_specs=row_blk, out_shape=jax.ShapeDtypeStruct((S, D_MODEL), F32), scratch_shapes=scratch,
        name="attn_prompt",
        compiler_params=pltpu.CompilerParams(dimension_semantics=("arbitrary",), vmem_limit_bytes=VMEM_LIMIT),
    )(qT, gT, res, kb, vT, km, wo)


def _pair_rms(x, g2):
    lo = lax.broadcasted_iota(jnp.int32, (x.shape[0], LANES), 1) < HEAD_DIM
    out = []
    for c in range(0, x.shape[1], LANES):
        xt = x[:, c:c + LANES]
        sq = xt * xt
        s_lo = jnp.sum(jnp.where(lo, sq, 0.0), axis=-1, keepdims=True)
        s_hi = jnp.sum(jnp.where(lo, 0.0, sq), axis=-1, keepdims=True)
        r = jnp.where(lo, lax.rsqrt(s_lo / HEAD_DIM + NORM_EPS), lax.rsqrt(s_hi / HEAD_DIM + NORM_EPS))
        out.append(xt * r * g2)
    return jnp.concatenate(out, axis=-1)


def _l1_proj_nat_kernel(x_ref, ng_ref, w_ref, qg_ref, kg_ref, q_ref, k_ref, v_ref, g_ref):
    xn = _rms_rows(x_ref[...], ng_ref[...]).astype(BF16)
    u = jnp.dot(xn, w_ref[...], preferred_element_type=F32)
    q_ref[...] = _pair_rms(u[:, R_Q:R_K], qg_ref[...]) * SCALE
    k_ref[...] = _pair_rms(u[:, R_K:R_V], kg_ref[...])
    v_ref[...] = u[:, R_V:R_G]
    g_ref[...] = _silu(u[:, R_G:IN1_ROWS])


def _l1_proj_nat(x, ng, w, qg2, kg2):
    m = x.shape[0]
    shapes = ((m, ATT_WIDTH), (m, KV_WIDTH), (m, KV_WIDTH), (m, ATT_WIDTH))
    return pl.pallas_call(
        _l1_proj_nat_kernel, grid=(1,),
        in_specs=[_const_spec(x.shape), _const_spec(ng.shape), _const_spec(w.shape), _const_spec(qg2.shape),
                  _const_spec(kg2.shape)],
        out_specs=tuple(_const_spec(s) for s in shapes),
        out_shape=tuple(jax.ShapeDtypeStruct(s, F32) for s in shapes), name="l1_proj_nat",
        compiler_params=pltpu.CompilerParams(vmem_limit_bytes=VMEM_LIMIT),
    )(x, ng, w, qg2, kg2)


def _attn_sample_kernel(n_tok, bps, pt_ref, q_ref, kn_ref, vn_ref, *refs):
    ppb = CHUNK // PAGE
    npg = bps * ppb
    k_refs, v_refs, o_ref = refs[:npg], refs[npg:2 * npg], refs[2 * npg]
    qh_s, new_s, s_s, m_s, l_s, g_s, o_s, oh_s = refs[2 * npg + 1:]
    step = pl.program_id(1)
    NB = m_s.shape[0] - 1
    R = 8
    gsz = ATT_HEADS // KV_HEADS
    GR = gsz * R
    hd = HEAD_DIM

    @pl.when(step == 0)
    def _():
        q = q_ref[0]
        for hq in range(ATT_HEADS):
            kvh, g = divmod(hq, gsz)
            qh_s[kvh, g * R:(g + 1) * R, :] = q[:, hq * hd:(hq + 1) * hd]

    def partial_softmax(kvh, s, v_h, slot, v_is_transposed):
        m = jnp.max(s, axis=-1, keepdims=True)
        p = jnp.exp(s - m)
        m_s[slot, kvh] = jnp.broadcast_to(m, (GR, hd))
        l_s[slot, kvh] = jnp.broadcast_to(jnp.sum(p, axis=-1, keepdims=True), (GR, hd))
        pv_dims = _NT if v_is_transposed else (((1,), (0,)), ((), ()))
        o_s[slot, kvh] = lax.dot_general(p.astype(BF16), v_h, pv_dims, preferred_element_type=F32)

    def head_T(page_refs, pages, kvh):
        return jnp.concatenate([page_refs[j][0, kvh] for j in pages], axis=-1).astype(BF16)

    for kvh in range(KV_HEADS):
        s_s[kvh] = jnp.dot(qh_s[kvh].astype(BF16), head_T(k_refs, range(npg), kvh), preferred_element_type=F32)
    for kvh in range(KV_HEADS):
        for i in range(bps):
            blk = step * bps + i
            s = s_s[kvh, :, i * CHUNK:(i + 1) * CHUNK]
            g_s[blk, kvh] = jnp.broadcast_to(jnp.sum(s, axis=-1, keepdims=True), (GR, hd))
            partial_softmax(kvh, s, head_T(v_refs, range(i * ppb, (i + 1) * ppb), kvh), blk, True)

    @pl.when(step == pl.num_programs(1) - 1)
    def _():
        new_s[...] = jnp.zeros(new_s.shape, F32)
        kn = kn_ref[0]
        vn = vn_ref[0]
        row_tok = lax.broadcasted_iota(jnp.int32, (GR, LANES), 0) % R
        key = lax.broadcasted_iota(jnp.int32, (GR, LANES), 1)
        own_ok = jnp.logical_and(key <= row_tok, key < n_tok)
        for kvh in range(KV_HEADS):
            new_s[0, kvh, 0:R, :] = kn[:, kvh * hd:(kvh + 1) * hd]
            new_s[1, kvh, 0:R, :] = vn[:, kvh * hd:(kvh + 1) * hd]
            qb = qh_s[kvh].astype(BF16)
            s = lax.dot_general(qb, new_s[0, kvh].astype(BF16), _NT, preferred_element_type=F32)
            partial_softmax(kvh, jnp.where(own_ok, s, MASKED), new_s[1, kvh].astype(BF16), NB, False)

            v1 = v2 = v3 = jnp.full((GR, hd), -jnp.inf, F32)
            i1 = i2 = i3 = jnp.full((GR, hd), -1.0, F32)
            for b in range(NB):
                g = g_s[b, kvh]
                c1, c2, c3 = g > v1, g > v2, g > v3
                v3, i3 = jnp.where(c2, v2, jnp.where(c3, g, v3)), jnp.where(c2, i2, jnp.where(c3, float(b), i3))
                v2, i2 = jnp.where(c1, v1, jnp.where(c2, g, v2)), jnp.where(c1, i1, jnp.where(c2, float(b), i2))
                v1, i1 = jnp.where(c1, g, v1), jnp.where(c1, float(b), i1)
            picked = lambda b: jnp.logical_or(jnp.logical_or(i1 == float(b), i2 == float(b)), i3 == float(b))
            m_tot = m_s[NB, kvh]
            for b in range(NB):
                m_tot = jnp.maximum(m_tot, jnp.where(picked(b), m_s[b, kvh], MASKED))
            w = jnp.exp(m_s[NB, kvh] - m_tot)
            num = w * o_s[NB, kvh]
            den = w * l_s[NB, kvh]
            for b in range(NB):
                w = jnp.where(picked(b), jnp.exp(m_s[b, kvh] - m_tot), 0.0)
                num = num + w * o_s[b, kvh]
                den = den + w * l_s[b, kvh]
            oh_s[kvh] = num / den
        for tile in range(ATT_HEADS // 2):
            pieces = []
            for hq in (2 * tile, 2 * tile + 1):
                kvh, g = divmod(hq, gsz)
                pieces.append(oh_s[kvh, g * R:(g + 1) * R, :])
            o_ref[0, :, tile * LANES:(tile + 1) * LANES] = jnp.concatenate(pieces, axis=-1)


def _attn_sample(page_table, q8, kn8, vn8, cache_kT, cache_vT, n_tok):
    B, n_pages = page_table.shape
    ppb = CHUNK // PAGE
    assert n_pages % ppb == 0, "the new tokens' block is assumed to hold no cached keys"
    nblk = n_pages // ppb
    bps = SAMPLE_BLOCKS_PER_STEP if nblk % SAMPLE_BLOCKS_PER_STEP == 0 else 1
    npg = bps * ppb
    GR = ATT_HEADS // KV_HEADS * 8
    seq_blk = lambda w: pl.BlockSpec((1, 8, w), lambda b, s, pt: (b, 0, 0))
    page_blk = lambda i: pl.BlockSpec((1, KV_HEADS, HEAD_DIM, PAGE), lambda b, s, pt: (pt[b, npg * s + i], 0, 0, 0))
    pages = [page_blk(i) for i in range(npg)]
    part = lambda: pltpu.VMEM((nblk + 1, KV_HEADS, GR, HEAD_DIM), F32)
    grid_spec = pltpu.PrefetchScalarGridSpec(
        num_scalar_prefetch=1, grid=(B, nblk // bps),
        in_specs=[seq_blk(ATT_WIDTH), seq_blk(KV_WIDTH), seq_blk(KV_WIDTH)] + pages + pages,
        out_specs=seq_blk(ATT_WIDTH),
        scratch_shapes=[pltpu.VMEM((KV_HEADS, GR, HEAD_DIM), F32), pltpu.VMEM((2, KV_HEADS, LANES, HEAD_DIM), F32),
                        pltpu.VMEM((KV_HEADS, GR, bps * CHUNK), F32), part(), part(), part(), part(),
                        pltpu.VMEM((KV_HEADS, GR, HEAD_DIM), F32)])
    return pl.pallas_call(
        functools.partial(_attn_sample_kernel, n_tok, bps), grid_spec=grid_spec,
        out_shape=jax.ShapeDtypeStruct((B, 8, ATT_WIDTH), F32), name="attn_sample",
        compiler_params=pltpu.CompilerParams(dimension_semantics=("arbitrary", "arbitrary"), vmem_limit_bytes=VMEM_LIMIT),
    )(page_table, q8, kn8, vn8, *([cache_kT] * npg), *([cache_vT] * npg))


def _gated_proj_res_kernel(r_ref, a_ref, g_ref, w_ref, o_ref):
    og = (a_ref[...] * g_ref[...]).astype(BF16)
    o_ref[...] = r_ref[...] + jnp.dot(og, w_ref[...], preferred_element_type=F32)


def _gated_proj_res(res, a, g, w):
    return pl.pallas_call(
        _gated_proj_res_kernel, grid=(1,),
        in_specs=[_const_spec(res.shape), _const_spec(a.shape), _const_spec(g.shape), _const_spec(w.shape)],
        out_specs=_const_spec(res.shape), out_shape=jax.ShapeDtypeStruct(res.shape, F32), name="gated_proj_res",
        compiler_params=pltpu.CompilerParams(vmem_limit_bytes=VMEM_LIMIT),
    )(res, a, g, w)


def _attention_layer(yp, ys, cache_k, cache_v, page_table, norm_g, w_in, qn_g, kn_g, w_out):
    S = yp.shape[1]
    B, T, _ = ys.shape
    ng = norm_g.reshape(1, -1)
    wo = w_out.astype(BF16)
    qT, gT, k_p, v_p, kb, vT, km = _l1_proj(yp.reshape(S, D_MODEL), ng, w_in.T.astype(BF16),
                                            qn_g.reshape(-1, 1), kn_g.reshape(-1, 1))
    yp2 = _attn_prompt(qT, gT, yp.reshape(S, D_MODEL), kb, vT, km.reshape(-1, KV_WIDTH), wo)

    ys2 = ys.reshape(B * T, D_MODEL)
    pair = lambda g: jnp.concatenate([g, g]).reshape(1, LANES)
    q_s, k_s, v_s, g_s = _l1_proj_nat(ys2, ng, w_in.astype(BF16), pair(qn_g), pair(kn_g))
    pad_tok = lambda a: jnp.pad(a.reshape(B, T, -1), ((0, 0), (0, 8 - T), (0, 0)))
    paged = lambda c: jnp.transpose(c, (0, 2, 3, 1))
    o_s = _attn_sample(page_table, pad_tok(q_s), pad_tok(k_s), pad_tok(v_s), paged(cache_k), paged(cache_v), T)
    ys3 = _gated_proj_res(ys2, o_s[:, :T].reshape(B * T, ATT_WIDTH), g_s, wo)
    tok_major = lambda aT: jnp.transpose(aT.reshape(KV_HEADS, HEAD_DIM, S), (2, 0, 1))[None]
    return (yp2.reshape(1, S, D_MODEL), ys3.reshape(B, T, D_MODEL), tok_major(k_p), tok_major(v_p),
            k_s.reshape(B, T, KV_HEADS, HEAD_DIM), v_s.reshape(B, T, KV_HEADS, HEAD_DIM))


def kernel(x_prompt, x_sample, state_ssm, state_ssd_conv, state_conf_conv, cache_k, cache_v, page_table, norm0_g, w_in0, ssd_conv_w, ssd_conv_b, ssd_dt_bias, ssd_a_log, ssd_d, ssd_norm_g, conf_conv_w, conf_conv_b, conf_ln_g, conf_ln_b, w_out0, norm1_g, w_in1, q_norm_g, k_norm_g, w_out1):
    p0 = _prep_l0_params(norm0_g[0], w_in0[0], ssd_conv_w[0], ssd_conv_b[0], ssd_dt_bias[0], ssd_a_log[0], ssd_d[0],
                         ssd_norm_g[0], conf_conv_w[0], conf_conv_b[0], conf_ln_g[0], conf_ln_b[0], w_out0[0])
    yp, ys, h_p, h_s, sh_p, sh_s, ch_p, ch_s = _hybrid_layer(
        x_prompt, x_sample, state_ssm[0], state_ssd_conv[0], state_conf_conv[0], p0)
    yp, ys, k_p, v_p, k_s, v_s = _attention_layer(yp, ys, cache_k[0], cache_v[0], page_table, norm1_g[0], w_in1[0],
                                                  q_norm_g[0], k_norm_g[0], w_out1[0])
    return (yp, ys, h_p[None], h_s[None], sh_p[None], sh_s[None], ch_p[None], ch_s[None],
            k_p[None], v_p[None], k_s[None], v_s[None])
```

```python
import functools

import jax
import jax.numpy as jnp
from jax import lax
from jax.experimental import pallas as pl
from jax.experimental.pallas import tpu as pltpu

F32 = jnp.float32
BF16 = jnp.bfloat16
NORM_EPS = 1e-6
MASKED = -1e30
LANES = 128
SUBLANES = 8
CHUNK = 256
D_MODEL = 1024
SSD_HEADS = 16
SSD_HEAD_DIM = 64
SSD_GROUPS = 4
SSD_STATE = 128
SSD_CONV = 4
SSD_CONV_DIM = 2048
CONF_CONV = 31
ATT_HEADS = 16
KV_HEADS = 4
HEAD_DIM = 64
KV_WIDTH = KV_HEADS * HEAD_DIM
MOBA_TOP_K = 3
PAGE = 128
C_Z, C_XBC, C_GA, C_GB, C_CG, C_DT, IN0_PAD = 0, 1024, 3072, 4096, 5120, 6144, 6272
SSD_HIST_PAD = 8
CONF_HIST_PAD = 32
SAMPLE_CHUNK = 128
SAMPLE_BLOCKS_PER_STEP = 8
VMEM_LIMIT = 56 * 1024 * 1024

_NT = (((1,), (1,)), ((), ()))


def _sigmoid(x):
    return 1.0 / (1.0 + jnp.exp(-x))


def _silu(x):
    return x * _sigmoid(x)


def _softplus(x):
    return jnp.maximum(x, 0.0) + jnp.log1p(jnp.exp(-jnp.abs(x)))


def _rms_rows(x, g):
    ms = jnp.mean(x * x, axis=-1, keepdims=True)
    return x * lax.rsqrt(ms + NORM_EPS) * g


def _const_spec(shape):
    nd = len(shape)
    return pl.BlockSpec(shape, lambda *_: (0,) * nd)


def _ssd_conv_silu(buf_ref, cw_ref, cb_ref, rows, out_ref):
    for c in range(0, SSD_CONV_DIM, 512):
        acc = cb_ref[:, c:c + 512]
        for k in range(SSD_CONV):
            o = SSD_HIST_PAD - (SSD_CONV - 1) + k
            acc = acc + cw_ref[k:k + 1, c:c + 512] * buf_ref[o:o + rows, c:c + 512]
        out_ref[:, c:c + 512] = _silu(acc)


def _gated_group_norm(y, z, sng_ref, yc_ref):
    y = y * _silu(z)
    gw = D_MODEL // SSD_GROUPS
    for g in range(SSD_GROUPS):
        yg = y[:, g * gw:(g + 1) * gw]
        r = lax.rsqrt(jnp.mean(yg * yg, axis=-1, keepdims=True) + NORM_EPS)
        yc_ref[:, g * gw:(g + 1) * gw] = (yg * r * sng_ref[:, g * gw:(g + 1) * gw]).astype(yc_ref.dtype)


CONV_WIN_EXTRA = (CONF_CONV - 1) // SUBLANES * SUBLANES


def _conv_window_scratch(rows):
    return pltpu.VMEM((SUBLANES, min(rows, 128) + CONV_WIN_EXTRA, LANES), F32)


def _conformer(gl_ref, rows, cgate, ccw_ref, ccb_ref, lng_ref, lnb_ref, conv_ref, win_ref, yc_ref):
    first = CONF_HIST_PAD - (CONF_CONV - 1)
    rb = min(rows, 128)
    for c in range(0, D_MODEL, LANES):
        for r0 in range(0, rows, rb):
            acc = jnp.broadcast_to(ccb_ref[:, c:c + LANES], (rb, LANES))
            for phase in range(SUBLANES):
                taps = [k for k in range(CONF_CONV) if (first + k) % SUBLANES == phase]
                if not taps:
                    continue
                lo, hi = first + taps[0], first + taps[-1]
                n_win = hi - lo + rb
                win_ref[phase, 0:n_win, :] = gl_ref[r0 + lo:r0 + hi + rb, c:c + LANES]
                for k in taps:
                    off = first + k - lo
                    acc = acc + ccw_ref[k:k + 1, c:c + LANES] * win_ref[phase, off:off + rb, :]
            conv_ref[r0:r0 + rb, c:c + LANES] = acc
    cv = conv_ref[...]
    mu = jnp.mean(cv, axis=-1, keepdims=True)
    xc = cv - mu
    var = jnp.mean(xc * xc, axis=-1, keepdims=True)
    cn = xc * lax.rsqrt(var + NORM_EPS) * lng_ref[...] + lnb_ref[...]
    yc_ref[:, D_MODEL:2 * D_MODEL] = (_silu(cn) * _silu(cgate)).astype(yc_ref.dtype)


def _ssd_chunk(L, act_ref, dt, alog_ref, dcol_ref, h_in_ref, h_out_ref, yT_ref):
    a = dt * (-jnp.exp(alog_ref[...]))
    r_i = lax.broadcasted_iota(jnp.int32, (L, L), 0)
    c_i = lax.broadcasted_iota(jnp.int32, (L, L), 1)
    tri = (c_i <= r_i).astype(F32)
    acs = jnp.dot(tri, a, precision=lax.Precision.HIGHEST, preferred_element_type=F32)
    acsT = acs.T
    dtT = dt.T
    last = acsT[:, L - 1:L]
    exp_acsT = jnp.exp(acsT)
    dec_endT = jnp.exp(last - acsT)
    chunk_dec = jnp.exp(last)
    causalT = r_i <= c_i

    xT = act_ref[:, 0:D_MODEL].T
    P = SSD_HEAD_DIM
    hpg = SSD_HEADS // SSD_GROUPS
    gp = hpg * P
    for g in range(SSD_GROUPS):
        b_g = act_ref[:, D_MODEL + g * SSD_STATE:D_MODEL + (g + 1) * SSD_STATE].astype(BF16)
        c_g = act_ref[:, D_MODEL + (SSD_GROUPS + g) * SSD_STATE:D_MODEL + (SSD_GROUPS + g + 1) * SSD_STATE].astype(BF16)
        cbT = lax.dot_general(b_g, c_g, _NT, preferred_element_type=F32)
        h_g = h_in_ref[g * gp:(g + 1) * gp, :]
        y_offT = lax.dot_general(h_g.astype(BF16), c_g, _NT, preferred_element_type=F32)
        st_lhs = []
        for r in range(hpg):
            h = g * hpg + r
            rows = slice(h * P, (h + 1) * P)
            xT_h = xT[rows, :]
            xdtT_h = xT_h * dtT[h:h + 1, :]
            seg = acsT[h:h + 1, :] - acs[:, h:h + 1]
            mT = (jnp.exp(jnp.where(causalT, seg, -jnp.inf)) * cbT).astype(BF16)
            y_dT = jnp.dot(xdtT_h.astype(BF16), mT, preferred_element_type=F32)
            y_oT = y_offT[r * P:(r + 1) * P, :] * exp_acsT[h:h + 1, :]
            yT_ref[rows, :] = y_dT + y_oT + dcol_ref[rows, :] * xT_h
            st_lhs.append((xdtT_h * dec_endT[h:h + 1, :]).astype(BF16))
        st = jnp.dot(jnp.concatenate(st_lhs, axis=0), b_g, preferred_element_type=F32)
        for r in range(hpg):
            h = g * hpg + r
            h_out_ref[h * P:(h + 1) * P, :] = h_g[r * P:(r + 1) * P, :] * chunk_dec[h:h + 1, :] + st[r * P:(r + 1) * P, :]


def _l0_prompt_kernel(x_ref, ng_ref, w_ref, cw_ref, cb_ref, dtb_ref, alog_ref, dcol_ref, sng_ref,
                      ccw_ref, ccb_ref, lng_ref, lnb_ref, wo_ref, h0_ref, sh0_ref, ch0_ref,
                      y_ref, hT_ref, sh_ref, chh_ref,
                      u_s, xbc_s, gl_s, h_s, act_s, yT_s, conv_s, win_s, yc_s):
    i = pl.program_id(0)
    L = CHUNK

    @pl.when(i == 0)
    def _():
        h_s[...] = h0_ref[...]
        xbc_s[0:SSD_HIST_PAD, :] = sh0_ref[...]
        gl_s[0:CONF_HIST_PAD, :] = ch0_ref[...]

    x = x_ref[...]
    xn = _rms_rows(x, ng_ref[...]).astype(BF16)
    nblk = IN0_PAD // 896
    for c in range(nblk):
        u_s[:, c * 896:(c + 1) * 896] = jnp.dot(xn, w_ref[:, c * 896:(c + 1) * 896], preferred_element_type=F32)

    xbc_s[SSD_HIST_PAD:SSD_HIST_PAD + L, :] = u_s[:, C_XBC:C_XBC + SSD_CONV_DIM]
    _ssd_conv_silu(xbc_s, cw_ref, cb_ref, L, act_s)

    dt = _softplus(u_s[:, C_DT:C_DT + LANES] + dtb_ref[...])
    _ssd_chunk(L, act_s, dt, alog_ref, dcol_ref, h_s, h_s, yT_s)
    _gated_group_norm(yT_s[...].T, u_s[:, C_Z:C_Z + D_MODEL], sng_ref, yc_s)

    gl_s[CONF_HIST_PAD:CONF_HIST_PAD + L, :] = u_s[:, C_GA:C_GA + D_MODEL] * _sigmoid(u_s[:, C_GB:C_GB + D_MODEL])
    _conformer(gl_s, L, u_s[:, C_CG:C_CG + D_MODEL], ccw_ref, ccb_ref, lng_ref, lnb_ref, conv_s, win_s, yc_s)

    y_ref[...] = x + jnp.dot(yc_s[...], wo_ref[...], preferred_element_type=F32)

    xbc_s[0:SSD_HIST_PAD, :] = xbc_s[L:L + SSD_HIST_PAD, :]
    gl_s[0:CONF_HIST_PAD, :] = gl_s[L:L + CONF_HIST_PAD, :]

    @pl.when(i == pl.num_programs(0) - 1)
    def _():
        hT_ref[...] = h_s[...]
        sh_ref[...] = xbc_s[0:SSD_HIST_PAD, :]
        chh_ref[...] = gl_s[0:CONF_HIST_PAD, :]


def _l0_prompt(x, p, h0, sh0, ch0):
    S = x.shape[0]
    assert S % CHUNK == 0
    nc = S // CHUNK
    row_blk = lambda w: pl.BlockSpec((CHUNK, w), lambda i: (i, 0))
    in_specs = [row_blk(D_MODEL), _const_spec((1, D_MODEL)), _const_spec((D_MODEL, IN0_PAD)),
                _const_spec((SSD_CONV, SSD_CONV_DIM)), _const_spec((1, SSD_CONV_DIM)),
                _const_spec((1, LANES)), _const_spec((1, LANES)), _const_spec((D_MODEL, 1)), _const_spec((1, D_MODEL)),
                _const_spec((CONF_CONV, D_MODEL)), _const_spec((1, D_MODEL)), _const_spec((1, D_MODEL)),
                _const_spec((1, D_MODEL)), _const_spec((2 * D_MODEL, D_MODEL)),
                _const_spec((D_MODEL, SSD_STATE)), _const_spec((SSD_HIST_PAD, SSD_CONV_DIM)),
                _const_spec((CONF_HIST_PAD, D_MODEL))]
    out_shape = (jax.ShapeDtypeStruct((S, D_MODEL), F32), jax.ShapeDtypeStruct((D_MODEL, SSD_STATE), F32),
                 jax.ShapeDtypeStruct((SSD_HIST_PAD, SSD_CONV_DIM), F32), jax.ShapeDtypeStruct((CONF_HIST_PAD, D_MODEL), F32))
    out_specs = (row_blk(D_MODEL), _const_spec((D_MODEL, SSD_STATE)), _const_spec((SSD_HIST_PAD, SSD_CONV_DIM)),
                 _const_spec((CONF_HIST_PAD, D_MODEL)))
    scratch = [pltpu.VMEM((CHUNK, IN0_PAD), F32), pltpu.VMEM((SSD_HIST_PAD + CHUNK, SSD_CONV_DIM), F32),
               pltpu.VMEM((CONF_HIST_PAD + CHUNK, D_MODEL), F32), pltpu.VMEM((D_MODEL, SSD_STATE), F32),
               pltpu.VMEM((CHUNK, SSD_CONV_DIM), F32), pltpu.VMEM((D_MODEL, CHUNK), F32),
               pltpu.VMEM((CHUNK, D_MODEL), F32), _conv_window_scratch(CHUNK), pltpu.VMEM((CHUNK, 2 * D_MODEL), BF16)]
    return pl.pallas_call(
        _l0_prompt_kernel, grid=(nc,), in_specs=in_specs, out_specs=out_specs, out_shape=out_shape,
        scratch_shapes=scratch, name="l0_prompt",
        compiler_params=pltpu.CompilerParams(dimension_semantics=("arbitrary",), vmem_limit_bytes=VMEM_LIMIT),
    )(x, p["ng"], p["w_in"], p["cw"], p["cb"], p["dtb"], p["alog"], p["dcol"], p["sng"],
      p["ccw"], p["ccb"], p["lng"], p["lnb"], p["w_out"], h0, sh0, ch0)


def _norm_proj_kernel(x_ref, g_ref, w_ref, o_ref):
    xn = _rms_rows(x_ref[...], g_ref[...]).astype(BF16)
    o_ref[...] = jnp.dot(xn, w_ref[...], preferred_element_type=F32)


def _norm_proj(x, g, w):
    m, n = x.shape[0], w.shape[1]
    return pl.pallas_call(
        _norm_proj_kernel, grid=(1,),
        in_specs=[_const_spec(x.shape), _const_spec(g.shape), _const_spec(w.shape)],
        out_specs=_const_spec((m, n)), out_shape=jax.ShapeDtypeStruct((m, n), F32), name="norm_proj",
        compiler_params=pltpu.CompilerParams(vmem_limit_bytes=VMEM_LIMIT),
    )(x, g, w)


def _proj_res_kernel(r_ref, a_ref, w_ref, o_ref):
    o_ref[...] = r_ref[...] + jnp.dot(a_ref[...].astype(BF16), w_ref[...], preferred_element_type=F32)


def _proj_res(res, a, w):
    return pl.pallas_call(
        _proj_res_kernel, grid=(1,),
        in_specs=[_const_spec(res.shape), _const_spec(a.shape), _const_spec(w.shape)],
        out_specs=_const_spec(res.shape), out_shape=jax.ShapeDtypeStruct(res.shape, F32), name="proj_res",
        compiler_params=pltpu.CompilerParams(vmem_limit_bytes=VMEM_LIMIT),
    )(res, a, w)


def _l0_sample_kernel(n_tok, u_ref, cw_ref, cb_ref, dtb_ref, alog_ref, dcol_ref, sng_ref,
                      ccw_ref, ccb_ref, lng_ref, lnb_ref, h0_ref, sh0_ref, ch0_ref,
                      yc_ref, hT_ref, sh_ref, chh_ref,
                      xbc_s, gl_s, act_s, yT_s, conv_s, win_s):
    R = 8
    L = SAMPLE_CHUNK
    u = u_ref[0]
    xbc_s[0:SSD_HIST_PAD, :] = sh0_ref[0]
    xbc_s[SSD_HIST_PAD:SSD_HIST_PAD + R, :] = u[:, C_XBC:C_XBC + SSD_CONV_DIM]
    act_s[R:L, :] = jnp.zeros((L - R, SSD_CONV_DIM), F32)
    _ssd_conv_silu(xbc_s, cw_ref, cb_ref, R, act_s.at[0:R])

    dt = _softplus(u[:, C_DT:C_DT + LANES] + dtb_ref[...])
    dt = jnp.where(lax.broadcasted_iota(jnp.int32, (R, LANES), 0) < n_tok, dt, 0.0)
    dt = jnp.concatenate([dt, jnp.zeros((L - R, LANES), F32)], axis=0)
    _ssd_chunk(L, act_s, dt, alog_ref, dcol_ref, h0_ref.at[0], hT_ref.at[0], yT_s)
    y = yT_s[...].T[0:R, :]
    _gated_group_norm(y, u[:, C_Z:C_Z + D_MODEL], sng_ref, yc_ref.at[0])

    gl_s[0:CONF_HIST_PAD, :] = ch0_ref[0]
    gl_s[CONF_HIST_PAD:CONF_HIST_PAD + R, :] = u[:, C_GA:C_GA + D_MODEL] * _sigmoid(u[:, C_GB:C_GB + D_MODEL])
    _conformer(gl_s, R, u[:, C_CG:C_CG + D_MODEL], ccw_ref, ccb_ref, lng_ref, lnb_ref, conv_s, win_s, yc_ref.at[0])

    sh_ref[0] = xbc_s[...]
    chh_ref[0] = gl_s[...]


def _l0_sample(u, p, h0, sh0, ch0, n_tok):
    B = u.shape[0]
    seq_blk = lambda *s: pl.BlockSpec((1,) + s, lambda b: (b,) + (0,) * len(s))
    in_specs = [seq_blk(8, IN0_PAD),
                _const_spec((SSD_CONV, SSD_CONV_DIM)), _const_spec((1, SSD_CONV_DIM)),
                _const_spec((1, LANES)), _const_spec((1, LANES)), _const_spec((D_MODEL, 1)), _const_spec((1, D_MODEL)),
                _const_spec((CONF_CONV, D_MODEL)), _const_spec((1, D_MODEL)), _const_spec((1, D_MODEL)),
                _const_spec((1, D_MODEL)),
                seq_blk(D_MODEL, SSD_STATE), seq_blk(SSD_HIST_PAD, SSD_CONV_DIM), seq_blk(CONF_HIST_PAD, D_MODEL)]
    out_shape = (jax.ShapeDtypeStruct((B, 8, 2 * D_MODEL), F32), jax.ShapeDtypeStruct((B, D_MODEL, SSD_STATE), F32),
                 jax.ShapeDtypeStruct((B, SSD_HIST_PAD + 8, SSD_CONV_DIM), F32),
                 jax.ShapeDtypeStruct((B, CONF_HIST_PAD + 8, D_MODEL), F32))
    out_specs = (seq_blk(8, 2 * D_MODEL), seq_blk(D_MODEL, SSD_STATE), seq_blk(SSD_HIST_PAD + 8, SSD_CONV_DIM),
                 seq_blk(CONF_HIST_PAD + 8, D_MODEL))
    scratch = [pltpu.VMEM((SSD_HIST_PAD + 8, SSD_CONV_DIM), F32), pltpu.VMEM((CONF_HIST_PAD + 8, D_MODEL), F32),
               pltpu.VMEM((SAMPLE_CHUNK, SSD_CONV_DIM), F32),
               pltpu.VMEM((D_MODEL, SAMPLE_CHUNK), F32), pltpu.VMEM((8, D_MODEL), F32), _conv_window_scratch(8)]
    return pl.pallas_call(
        functools.partial(_l0_sample_kernel, n_tok), grid=(B,), in_specs=in_specs, out_specs=out_specs,
        out_shape=out_shape, scratch_shapes=scratch, name="l0_sample",
        compiler_params=pltpu.CompilerParams(dimension_semantics=("arbitrary",), vmem_limit_bytes=VMEM_LIMIT),
    )(u, p["cw"], p["cb"], p["dtb"], p["alog"], p["dcol"], p["sng"], p["ccw"], p["ccb"], p["lng"], p["lnb"],
      h0, sh0, ch0)


def _prep_l0_params(norm_g, w_in, conv_w, conv_b, dt_bias, a_log, d_skip, ssd_norm_g,
                    cconv_w, cconv_b, cln_g, cln_b, w_out):
    o_dt = D_MODEL + SSD_CONV_DIM
    w = jnp.concatenate([w_in[:, :o_dt], w_in[:, o_dt + SSD_HEADS:], w_in[:, o_dt:o_dt + SSD_HEADS],
                         jnp.zeros((D_MODEL, LANES - SSD_HEADS), w_in.dtype)], axis=1)
    lane_pad = lambda v: jnp.pad(v.astype(F32), (0, LANES - SSD_HEADS)).reshape(1, LANES)
    return dict(ng=norm_g.reshape(1, -1), w_in=w.astype(BF16), cw=conv_w, cb=conv_b.reshape(1, -1),
                dtb=lane_pad(dt_bias), alog=lane_pad(a_log),
                dcol=jnp.repeat(d_skip.astype(F32), SSD_HEAD_DIM).reshape(-1, 1), sng=ssd_norm_g.reshape(1, -1),
                ccw=cconv_w, ccb=cconv_b.reshape(1, -1), lng=cln_g.reshape(1, -1), lnb=cln_b.reshape(1, -1),
                w_out=w_out.astype(BF16))


def _hybrid_layer(x_p, x_s, state_ssm, ssd_hist, conf_hist, p):
    S = x_p.shape[1]
    B, T, _ = x_s.shape
    H, P, N = SSD_HEADS, SSD_HEAD_DIM, SSD_STATE
    zeros = lambda *s: jnp.zeros(s, F32)
    yp, h_p, sh_p, ch_p = _l0_prompt(x_p.reshape(S, D_MODEL), p, zeros(D_MODEL, N), zeros(SSD_HIST_PAD, SSD_CONV_DIM),
                                     zeros(CONF_HIST_PAD, D_MODEL))
    xs2 = x_s.reshape(B * T, D_MODEL)
    u_s = _norm_proj(xs2, p["ng"], p["w_in"]).reshape(B, T, IN0_PAD)
    u_s = jnp.pad(u_s, ((0, 0), (0, 8 - T), (0, 0)))
    sh0 = jnp.pad(ssd_hist, ((0, 0), (SSD_HIST_PAD - (SSD_CONV - 1), 0), (0, 0)))
    ch0 = jnp.pad(conf_hist, ((0, 0), (CONF_HIST_PAD - (CONF_CONV - 1), 0), (0, 0)))
    yc_s, h_s, sh_s, ch_s = _l0_sample(u_s, p, state_ssm.reshape(B, H * P, N), sh0, ch0, T)
    ys = _proj_res(xs2, yc_s[:, :T].reshape(B * T, 2 * D_MODEL), p["w_out"])
    return (yp.reshape(1, S, D_MODEL), ys.reshape(B, T, D_MODEL),
            h_p.reshape(1, H, P, N), h_s.reshape(B, H, P, N),
            sh_p[None, SSD_HIST_PAD - (SSD_CONV - 1):], sh_s[:, SSD_HIST_PAD + T - (SSD_CONV - 1):SSD_HIST_PAD + T],
            ch_p[None, CONF_HIST_PAD - (CONF_CONV - 1):], ch_s[:, CONF_HIST_PAD + T - (CONF_CONV - 1):CONF_HIST_PAD + T])


ATT_WIDTH = ATT_HEADS * HEAD_DIM
R_Q, R_K, R_V, R_G, IN1_ROWS = 0, ATT_WIDTH, ATT_WIDTH + KV_WIDTH, ATT_WIDTH + 2 * KV_WIDTH, 2 * ATT_WIDTH + 2 * KV_WIDTH
SCALE = HEAD_DIM ** -0.5
LOG2_E = 1.4426950408889634
V_EXT = HEAD_DIM + 16


def _head_rms_cols(xT, g_col):
    out = []
    for h in range(xT.shape[0] // HEAD_DIM):
        xh = xT[h * HEAD_DIM:(h + 1) * HEAD_DIM, :]
        r = lax.rsqrt(jnp.mean(xh * xh, axis=0, keepdims=True) + NORM_EPS)
        out.append(xh * r * g_col)
    return out


def _l1_proj_kernel(x_ref, ng_ref, wT_ref, qg_ref, kg_ref, qT_ref, gT_ref, kn_ref, vn_ref, kb_ref, vT_ref, km_ref):
    xn = _rms_rows(x_ref[...], ng_ref[...]).astype(BF16)
    uT = lax.dot_general(wT_ref[...], xn, _NT, preferred_element_type=F32)
    for h, qh in enumerate(_head_rms_cols(uT[R_Q:R_K, :], qg_ref[...])):
        qT_ref[h * HEAD_DIM:(h + 1) * HEAD_DIM, :] = (qh * (SCALE * LOG2_E)).astype(BF16)
    kT = jnp.concatenate(_head_rms_cols(uT[R_K:R_V, :], kg_ref[...]), axis=0)
    kn_ref[...] = kT
    k_nat = kT.T
    kb_ref[0] = k_nat.astype(BF16)
    km_ref[0] = jnp.mean(k_nat, axis=0, keepdims=True)
    vT = uT[R_V:R_G, :]
    vn_ref[...] = vT
    for h in range(KV_HEADS):
        vT_ref[0, h * V_EXT:h * V_EXT + HEAD_DIM, :] = vT[h * HEAD_DIM:(h + 1) * HEAD_DIM, :].astype(BF16)
        vT_ref[0, h * V_EXT + HEAD_DIM:(h + 1) * V_EXT, :] = jnp.ones((V_EXT - HEAD_DIM, vT.shape[1]), BF16)
    gT_ref[...] = _silu(uT[R_G:IN1_ROWS, :])


def _l1_proj(x, ng, wT, qg_col, kg_col):
    S = x.shape[0]
    nb = S // CHUNK
    col_blk = lambda r: pl.BlockSpec((r, CHUNK), lambda i: (0, i))
    row_blk = lambda w: pl.BlockSpec((CHUNK, w), lambda i: (i, 0))
    blk3 = lambda a, b: pl.BlockSpec((1, a, b), lambda i: (i, 0, 0))
    out_shape = (jax.ShapeDtypeStruct((ATT_WIDTH, S), BF16), jax.ShapeDtypeStruct((ATT_WIDTH, S), F32),
                 jax.ShapeDtypeStruct((KV_WIDTH, S), F32), jax.ShapeDtypeStruct((KV_WIDTH, S), F32),
                 jax.ShapeDtypeStruct((nb, CHUNK, KV_WIDTH), BF16), jax.ShapeDtypeStruct((nb, KV_HEADS * V_EXT, CHUNK), BF16),
                 jax.ShapeDtypeStruct((nb, 1, KV_WIDTH), F32))
    out_specs = (col_blk(ATT_WIDTH), col_blk(ATT_WIDTH), col_blk(KV_WIDTH), col_blk(KV_WIDTH),
                 blk3(CHUNK, KV_WIDTH), blk3(KV_HEADS * V_EXT, CHUNK), blk3(1, KV_WIDTH))
    return pl.pallas_call(
        _l1_proj_kernel, grid=(nb,),
        in_specs=[row_blk(D_MODEL), _const_spec((1, D_MODEL)), _const_spec((IN1_ROWS, D_MODEL)),
                  _const_spec((HEAD_DIM, 1)), _const_spec((HEAD_DIM, 1))],
        out_specs=out_specs, out_shape=out_shape, name="l1_proj",
        compiler_params=pltpu.CompilerParams(dimension_semantics=("arbitrary",), vmem_limit_bytes=VMEM_LIMIT),
    )(x, ng, wT, qg_col, kg_col)


def _top_k_bias(gate, idx, n_valid_mask, axis):
    n = gate.shape[axis]
    g = jnp.where(n_valid_mask, gate, -jnp.inf)
    sel = jnp.zeros(gate.shape, F32)
    for _ in range(MOBA_TOP_K):
        mx = jnp.max(g, axis=axis, keepdims=True)
        first = jnp.min(jnp.where(g == mx, idx, n), axis=axis, keepdims=True)
        pick = jnp.logical_and(idx == first, mx > -jnp.inf)
        sel = jnp.where(pick, 1.0, sel)
        g = jnp.where(pick, -jnp.inf, g)
    return jnp.where(sel > 0.0, 0.0, MASKED)


def _attn_prompt_kernel(qT_ref, gT_ref, r_ref, kb_ref, vT_ref, km_ref, wo_ref, o_ref, qz_s, bias_s, m_s, a_s,
                        acc_s, og_s, s_s, p_s):
    t = pl.program_id(0)
    L = CHUNK
    nb = km_ref.shape[0]
    gsz = ATT_HEADS // KV_HEADS
    hd = HEAD_DIM

    zeros = jnp.zeros((hd, L), BF16)
    for hq in range(ATT_HEADS):
        q = qT_ref[hq * hd:(hq + 1) * hd, :]
        qz_s[hq] = jnp.concatenate([q, zeros] if (hq // gsz) % 2 == 0 else [zeros, q], axis=0)

    n_i = lax.broadcasted_iota(jnp.int32, (nb, L), 0)
    for hq in range(ATT_HEADS):
        j = (hq // gsz) // 2
        gate = jnp.dot(km_ref[:, j * LANES:(j + 1) * LANES].astype(BF16), qz_s[hq], preferred_element_type=F32)
        bias_s[hq] = _top_k_bias(gate, n_i, n_i < t, 0)

    def scores(hq, n):
        j = (hq // gsz) // 2
        return jnp.dot(kb_ref[n, :, j * LANES:(j + 1) * LANES], qz_s[hq], preferred_element_type=F32)

    def v_rows(hq, n):
        kvh = hq // gsz
        return vT_ref[n, kvh * V_EXT:(kvh + 1) * V_EXT, :]

    causalT = lax.broadcasted_iota(jnp.int32, (L, L), 0) <= lax.broadcasted_iota(jnp.int32, (L, L), 1)

    def stage_scores(n, half, own):
        keys = slice(half * L, (half + 1) * L)
        for hq in range(ATT_HEADS):
            s = scores(hq, n).astype(BF16)
            if own:
                s_s[hq, keys, :] = jnp.where(causalT, s, MASKED)
            else:
                s_s[hq, keys, :] = s + bias_s[hq, pl.ds(n, 1), :].astype(BF16)

    def softmax_and_pv(n, n_blocks, own):
        keys = slice(0, n_blocks * L)
        for hq in range(ATT_HEADS):
            m_new = jnp.max(s_s[hq, keys, :], axis=0, keepdims=True).astype(F32)
            if not own:
                m_old = m_s[hq]
                m_new = jnp.maximum(m_old, m_new)
                a_s[hq] = jnp.exp2(m_old - m_new)
            p_s[hq, keys, :] = jnp.exp2(s_s[hq, keys, :] - m_new.astype(BF16))
            m_s[hq] = m_new
        for hq in range(ATT_HEADS):
            rows = slice(hq * V_EXT, (hq + 1) * V_EXT)
            v = jnp.concatenate([v_rows(hq, n + b) for b in range(n_blocks)], axis=1)
            pv = jnp.dot(v, p_s[hq, keys, :], preferred_element_type=F32)
            acc_s[rows, :] = pv if own else a_s[hq] * acc_s[rows, :] + pv

    stage_scores(t, 0, True)
    softmax_and_pv(t, 1, True)

    def past_pair(i, carry):
        stage_scores(2 * i, 0, False)
        stage_scores(2 * i + 1, 1, False)
        softmax_and_pv(2 * i, 2, False)
        return carry

    lax.fori_loop(0, t // 2, past_pair, 0)

    @pl.when(t % 2 == 1)
    def _():
        stage_scores(t - 1, 0, False)
        softmax_and_pv(t - 1, 1, False)

    for hq in range(ATT_HEADS):
        rows = slice(hq * hd, (hq + 1) * hd)
        og_s[rows, :] = acc_s[hq * V_EXT:hq * V_EXT + hd, :] / acc_s[hq * V_EXT + hd:hq * V_EXT + hd + 1, :] * gT_ref[rows, :]
    og = og_s[...].T.astype(BF16)
    o_ref[...] = r_ref[...] + jnp.dot(og, wo_ref[...], preferred_element_type=F32)


def _attn_prompt(qT, gT, res, kb, vT, km, wo):
    S = res.shape[0]
    nb = S // CHUNK
    col_blk = pl.BlockSpec((ATT_WIDTH, CHUNK), lambda i: (0, i))
    row_blk = pl.BlockSpec((CHUNK, D_MODEL), lambda i: (i, 0))
    scratch = [pltpu.VMEM((ATT_HEADS, 2 * HEAD_DIM, CHUNK), BF16), pltpu.VMEM((ATT_HEADS, nb, CHUNK), F32),
               pltpu.VMEM((ATT_HEADS, 1, CHUNK), F32), pltpu.VMEM((ATT_HEADS, 1, CHUNK), F32),
               pltpu.VMEM((ATT_HEADS * V_EXT, CHUNK), F32), pltpu.VMEM((ATT_WIDTH, CHUNK), F32),
               pltpu.VMEM((ATT_HEADS, 2 * CHUNK, CHUNK), BF16), pltpu.VMEM((ATT_HEADS, 2 * CHUNK, CHUNK), BF16)]
    return pl.pallas_call(
        _attn_prompt_kernel, grid=(nb,),
        in_specs=[col_blk, col_blk, row_blk, _const_spec(kb.shape), _const_spec(vT.shape), _const_spec(km.shape),
                  _const_spec(wo.shape)],
        out_specs=row_blk, out_shape=jax.ShapeDtypeStruct((S, D_MODEL), F32), scratch_shapes=scratch,
        name="attn_prompt",
        compiler_params=pltpu.CompilerParams(dimension_semantics=("arbitrary",), vmem_limit_bytes=VMEM_LIMIT),
    )(qT, gT, res, kb, vT, km, wo)


def _pair_rms(x, g2):
    lo = lax.broadcasted_iota(jnp.int32, (x.shape[0], LANES), 1) < HEAD_DIM
    out = []
    for c in range(0, x.shape[1], LANES):
        xt = x[:, c:c + LANES]
        sq = xt * xt
        s_lo = jnp.sum(jnp.where(lo, sq, 0.0), axis=-1, keepdims=True)
        s_hi = jnp.sum(jnp.where(lo, 0.0, sq), axis=-1, keepdims=True)
        r = jnp.where(lo, lax.rsqrt(s_lo / HEAD_DIM + NORM_EPS), lax.rsqrt(s_hi / HEAD_DIM + NORM_EPS))
        out.append(xt * r * g2)
    return jnp.concatenate(out, axis=-1)


def _l1_proj_nat_kernel(x_ref, ng_ref, w_ref, qg_ref, kg_ref, q_ref, k_ref, v_ref, g_ref):
    xn = _rms_rows(x_ref[...], ng_ref[...]).astype(BF16)
    u = jnp.dot(xn, w_ref[...], preferred_element_type=F32)
    q_ref[...] = _pair_rms(u[:, R_Q:R_K], qg_ref[...]) * SCALE
    k_ref[...] = _pair_rms(u[:, R_K:R_V], kg_ref[...])
    v_ref[...] = u[:, R_V:R_G]
    g_ref[...] = _silu(u[:, R_G:IN1_ROWS])


def _l1_proj_nat(x, ng, w, qg2, kg2):
    m = x.shape[0]
    shapes = ((m, ATT_WIDTH), (m, KV_WIDTH), (m, KV_WIDTH), (m, ATT_WIDTH))
    return pl.pallas_call(
        _l1_proj_nat_kernel, grid=(1,),
        in_specs=[_const_spec(x.shape), _const_spec(ng.shape), _const_spec(w.shape), _const_spec(qg2.shape),
                  _const_spec(kg2.shape)],
        out_specs=tuple(_const_spec(s) for s in shapes),
        out_shape=tuple(jax.ShapeDtypeStruct(s, F32) for s in shapes), name="l1_proj_nat",
        compiler_params=pltpu.CompilerParams(vmem_limit_bytes=VMEM_LIMIT),
    )(x, ng, w, qg2, kg2)


def _attn_sample_kernel(n_tok, bps, pt_ref, q_ref, kn_ref, vn_ref, *refs):
    ppb = CHUNK // PAGE
    npg = bps * ppb
    k_refs, v_refs, o_ref = refs[:npg], refs[npg:2 * npg], refs[2 * npg]
    qh_s, new_s, s_s, m_s, l_s, g_s, o_s, oh_s = refs[2 * npg + 1:]
    step = pl.program_id(1)
    NB = m_s.shape[0] - 1
    R = 8
    gsz = ATT_HEADS // KV_HEADS
    GR = gsz * R
    hd = HEAD_DIM

    @pl.when(step == 0)
    def _():
        q = q_ref[0]
        for hq in range(ATT_HEADS):
            kvh, g = divmod(hq, gsz)
            qh_s[kvh, g * R:(g + 1) * R, :] = q[:, hq * hd:(hq + 1) * hd]

    def partial_softmax(kvh, s, v_h, slot, v_is_transposed):
        m = jnp.max(s, axis=-1, keepdims=True)
        p = jnp.exp(s - m)
        m_s[slot, kvh] = jnp.broadcast_to(m, (GR, hd))
        l_s[slot, kvh] = jnp.broadcast_to(jnp.sum(p, axis=-1, keepdims=True), (GR, hd))
        pv_dims = _NT if v_is_transposed else (((1,), (0,)), ((), ()))
        o_s[slot, kvh] = lax.dot_general(p.astype(BF16), v_h, pv_dims, preferred_element_type=F32)

    def head_T(page_refs, pages, kvh):
        return jnp.concatenate([page_refs[j][0, kvh] for j in pages], axis=-1).astype(BF16)

    for kvh in range(KV_HEADS):
        s_s[kvh] = jnp.dot(qh_s[kvh].astype(BF16), head_T(k_refs, range(npg), kvh), preferred_element_type=F32)
    for kvh in range(KV_HEADS):
        for i in range(bps):
            blk = step * bps + i
            s = s_s[kvh, :, i * CHUNK:(i + 1) * CHUNK]
            g_s[blk, kvh] = jnp.broadcast_to(jnp.sum(s, axis=-1, keepdims=True), (GR, hd))
            partial_softmax(kvh, s, head_T(v_refs, range(i * ppb, (i + 1) * ppb), kvh), blk, True)

    @pl.when(step == pl.num_programs(1) - 1)
    def _():
        new_s[...] = jnp.zeros(new_s.shape, F32)
        kn = kn_ref[0]
        vn = vn_ref[0]
        row_tok = lax.broadcasted_iota(jnp.int32, (GR, LANES), 0) % R
        key = lax.broadcasted_iota(jnp.int32, (GR, LANES), 1)
        own_ok = jnp.logical_and(key <= row_tok, key < n_tok)
        for kvh in range(KV_HEADS):
            new_s[0, kvh, 0:R, :] = kn[:, kvh * hd:(kvh + 1) * hd]
            new_s[1, kvh, 0:R, :] = vn[:, kvh * hd:(kvh + 1) * hd]
            qb = qh_s[kvh].astype(BF16)
            s = lax.dot_general(qb, new_s[0, kvh].astype(BF16), _NT, preferred_element_type=F32)
            partial_softmax(kvh, jnp.where(own_ok, s, MASKED), new_s[1, kvh].astype(BF16), NB, False)

            v1 = v2 = v3 = jnp.full((GR, hd), -jnp.inf, F32)
            i1 = i2 = i3 = jnp.full((GR, hd), -1.0, F32)
            for b in range(NB):
                g = g_s[b, kvh]
                c1, c2, c3 = g > v1, g > v2, g > v3
                v3, i3 = jnp.where(c2, v2, jnp.where(c3, g, v3)), jnp.where(c2, i2, jnp.where(c3, float(b), i3))
                v2, i2 = jnp.where(c1, v1, jnp.where(c2, g, v2)), jnp.where(c1, i1, jnp.where(c2, float(b), i2))
                v1, i1 = jnp.where(c1, g, v1), jnp.where(c1, float(b), i1)
            picked = lambda b: jnp.logical_or(jnp.logical_or(i1 == float(b), i2 == float(b)), i3 == float(b))
            m_tot = m_s[NB, kvh]
            for b in range(NB):
                m_tot = jnp.maximum(m_tot, jnp.where(picked(b), m_s[b, kvh], MASKED))
            w = jnp.exp(m_s[NB, kvh] - m_tot)
            num = w * o_s[NB, kvh]
            den = w * l_s[NB, kvh]
            for b in range(NB):
                w = jnp.where(picked(b), jnp.exp(m_s[b, kvh] - m_tot), 0.0)
                num = num + w * o_s[b, kvh]
                den = den + w * l_s[b, kvh]
            oh_s[kvh] = num / den
        for tile in range(ATT_HEADS // 2):
            pieces = []
            for hq in (2 * tile, 2 * tile + 1):
                kvh, g = divmod(hq, gsz)
                pieces.append(oh_s[kvh, g * R:(g + 1) * R, :])
            o_ref[0, :, tile * LANES:(tile + 1) * LANES] = jnp.concatenate(pieces, axis=-1)


def _attn_sample(page_table, q8, kn8, vn8, cache_kT, cache_vT, n_tok):
    B, n_pages = page_table.shape
    ppb = CHUNK // PAGE
    assert n_pages % ppb == 0, "the new tokens' block is assumed to hold no cached keys"
    nblk = n_pages // ppb
    bps = SAMPLE_BLOCKS_PER_STEP if nblk % SAMPLE_BLOCKS_PER_STEP == 0 else 1
    npg = bps * ppb
    GR = ATT_HEADS // KV_HEADS * 8
    seq_blk = lambda w: pl.BlockSpec((1, 8, w), lambda b, s, pt: (b, 0, 0))
    page_blk = lambda i: pl.BlockSpec((1, KV_HEADS, HEAD_DIM, PAGE), lambda b, s, pt: (pt[b, npg * s + i], 0, 0, 0))
    pages = [page_blk(i) for i in range(npg)]
    part = lambda: pltpu.VMEM((nblk + 1, KV_HEADS, GR, HEAD_DIM), F32)
    grid_spec = pltpu.PrefetchScalarGridSpec(
        num_scalar_prefetch=1, grid=(B, nblk // bps),
        in_specs=[seq_blk(ATT_WIDTH), seq_blk(KV_WIDTH), seq_blk(KV_WIDTH)] + pages + pages,
        out_specs=seq_blk(ATT_WIDTH),
        scratch_shapes=[pltpu.VMEM((KV_HEADS, GR, HEAD_DIM), F32), pltpu.VMEM((2, KV_HEADS, LANES, HEAD_DIM), F32),
                        pltpu.VMEM((KV_HEADS, GR, bps * CHUNK), F32), part(), part(), part(), part(),
                        pltpu.VMEM((KV_HEADS, GR, HEAD_DIM), F32)])
    return pl.pallas_call(
        functools.partial(_attn_sample_kernel, n_tok, bps), grid_spec=grid_spec,
        out_shape=jax.ShapeDtypeStruct((B, 8, ATT_WIDTH), F32), name="attn_sample",
        compiler_params=pltpu.CompilerParams(dimension_semantics=("arbitrary", "arbitrary"), vmem_limit_bytes=VMEM_LIMIT),
    )(page_table, q8, kn8, vn8, *([cache_kT] * npg), *([cache_vT] * npg))


def _gated_proj_res_kernel(r_ref, a_ref, g_ref, w_ref, o_ref):
    og = (a_ref[...] * g_ref[...]).astype(BF16)
    o_ref[...] = r_ref[...] + jnp.dot(og, w_ref[...], preferred_element_type=F32)


def _gated_proj_res(res, a, g, w):
    return pl.pallas_call(
        _gated_proj_res_kernel, grid=(1,),
        in_specs=[_const_spec(res.shape), _const_spec(a.shape), _const_spec(g.shape), _const_spec(w.shape)],
        out_specs=_const_spec(res.shape), out_shape=jax.ShapeDtypeStruct(res.shape, F32), name="gated_proj_res",
        compiler_params=pltpu.CompilerParams(vmem_limit_bytes=VMEM_LIMIT),
    )(res, a, g, w)


def _attention_layer(yp, ys, cache_k, cache_v, page_table, norm_g, w_in, qn_g, kn_g, w_out):
    S = yp.shape[1]
    B, T, _ = ys.shape
    ng = norm_g.reshape(1, -1)
    wo = w_out.astype(BF16)
    qT, gT, k_p, v_p, kb, vT, km = _l1_proj(yp.reshape(S, D_MODEL), ng, w_in.T.astype(BF16),
                                            qn_g.reshape(-1, 1), kn_g.reshape(-1, 1))
    yp2 = _attn_prompt(qT, gT, yp.reshape(S, D_MODEL), kb, vT, km.reshape(-1, KV_WIDTH), wo)

    ys2 = ys.reshape(B * T, D_MODEL)
    pair = lambda g: jnp.concatenate([g, g]).reshape(1, LANES)
    q_s, k_s, v_s, g_s = _l1_proj_nat(ys2, ng, w_in.astype(BF16), pair(qn_g), pair(kn_g))
    pad_tok = lambda a: jnp.pad(a.reshape(B, T, -1), ((0, 0), (0, 8 - T), (0, 0)))
    paged = lambda c: jnp.transpose(c, (0, 2, 3, 1))
    o_s = _attn_sample(page_table, pad_tok(q_s), pad_tok(k_s), pad_tok(v_s), paged(cache_k), paged(cache_v), T)
    ys3 = _gated_proj_res(ys2, o_s[:, :T].reshape(B * T, ATT_WIDTH), g_s, wo)
    tok_major = lambda aT: jnp.transpose(aT.reshape(KV_HEADS, HEAD_DIM, S), (2, 0, 1))[None]
    return (yp2.reshape(1, S, D_MODEL), ys3.reshape(B, T, D_MODEL), tok_major(k_p), tok_major(v_p),
            k_s.reshape(B, T, KV_HEADS, HEAD_DIM), v_s.reshape(B, T, KV_HEADS, HEAD_DIM))


def kernel(x_prompt, x_sample, state_ssm, state_ssd_conv, state_conf_conv, cache_k, cache_v, page_table, norm0_g, w_in0, ssd_conv_w, ssd_conv_b, ssd_dt_bias, ssd_a_log, ssd_d, ssd_norm_g, conf_conv_w, conf_conv_b, conf_ln_g, conf_ln_b, w_out0, norm1_g, w_in1, q_norm_g, k_norm_g, w_out1):
    p0 = _prep_l0_params(norm0_g[0], w_in0[0], ssd_conv_w[0], ssd_conv_b[0], ssd_dt_bias[0], ssd_a_log[0], ssd_d[0],
                         ssd_norm_g[0], conf_conv_w[0], conf_conv_b[0], conf_ln_g[0], conf_ln_b[0], w_out0[0])
    yp, ys, h_p, h_s, sh_p, sh_s, ch_p, ch_s = _hybrid_layer(
        x_prompt, x_sample, state_ssm[0], state_ssd_conv[0], state_conf_conv[0], p0)
    yp, ys, k_p, v_p, k_s, v_s = _attention_layer(yp, ys, cache_k[0], cache_v[0], page_table, norm1_g[0], w_in1[0],
                                                  q_norm_g[0], k_norm_g[0], w_out1[0])
    return (yp, ys, h_p[None], h_s[None], sh_p[None], sh_s[None], ch_p[None], ch_s[None],
            k_p[None], v_p[None], k_s[None], v_s[None])
```

```python
import functools

import jax
import jax.numpy as jnp
from jax import lax
from jax.experimental import pallas as pl
from jax.experimental.pallas import tpu as pltpu

F32 = jnp.float32
BF16 = jnp.bfloat16
NORM_EPS = 1e-6
MASKED = -1e30
LANES = 128
SUBLANES = 8
CHUNK = 256
D_MODEL = 1024
SSD_HEADS = 16
SSD_HEAD_DIM = 64
SSD_GROUPS = 4
SSD_STATE = 128
SSD_CONV = 4
SSD_CONV_DIM = 2048
CONF_CONV = 31
ATT_HEADS = 16
KV_HEADS = 4
HEAD_DIM = 64
KV_WIDTH = KV_HEADS * HEAD_DIM
MOBA_TOP_K = 3
PAGE = 128
C_Z, C_XBC, C_GA, C_GB, C_CG, C_DT, IN0_PAD = 0, 1024, 3072, 4096, 5120, 6144, 6272
SSD_HIST_PAD = 8
CONF_HIST_PAD = 32
SAMPLE_CHUNK = 128
SAMPLE_BLOCKS_PER_STEP = 8
VMEM_LIMIT = 56 * 1024 * 1024

_NT = (((1,), (1,)), ((), ()))


def _sigmoid(x):
    return 1.0 / (1.0 + jnp.exp(-x))


def _silu(x):
    return x * _sigmoid(x)


def _softplus(x):
    return jnp.maximum(x, 0.0) + jnp.log1p(jnp.exp(-jnp.abs(x)))


def _rms_rows(x, g):
    ms = jnp.mean(x * x, axis=-1, keepdims=True)
    return x * lax.rsqrt(ms + NORM_EPS) * g


def _const_spec(shape):
    nd = len(shape)
    return pl.BlockSpec(shape, lambda *_: (0,) * nd)


def _ssd_conv_silu(buf_ref, cw_ref, cb_ref, rows, out_ref):
    for c in range(0, SSD_CONV_DIM, 512):
        acc = cb_ref[:, c:c + 512]
        for k in range(SSD_CONV):
            o = SSD_HIST_PAD - (SSD_CONV - 1) + k
            acc = acc + cw_ref[k:k + 1, c:c + 512] * buf_ref[o:o + rows, c:c + 512]
        out_ref[:, c:c + 512] = _silu(acc)


def _gated_group_norm(y, z, sng_ref, yc_ref):
    y = y * _silu(z)
    gw = D_MODEL // SSD_GROUPS
    for g in range(SSD_GROUPS):
        yg = y[:, g * gw:(g + 1) * gw]
        r = lax.rsqrt(jnp.mean(yg * yg, axis=-1, keepdims=True) + NORM_EPS)
        yc_ref[:, g * gw:(g + 1) * gw] = (yg * r * sng_ref[:, g * gw:(g + 1) * gw]).astype(yc_ref.dtype)


CONV_WIN_EXTRA = (CONF_CONV - 1) // SUBLANES * SUBLANES


def _conv_window_scratch(rows):
    return pltpu.VMEM((SUBLANES, min(rows, 128) + CONV_WIN_EXTRA, LANES), F32)


def _conformer(gl_ref, rows, cgate, ccw_ref, ccb_ref, lng_ref, lnb_ref, conv_ref, win_ref, yc_ref):
    first = CONF_HIST_PAD - (CONF_CONV - 1)
    rb = min(rows, 128)
    for c in range(0, D_MODEL, LANES):
        for r0 in range(0, rows, rb):
            acc = jnp.broadcast_to(ccb_ref[:, c:c + LANES], (rb, LANES))
            for phase in range(SUBLANES):
                taps = [k for k in range(CONF_CONV) if (first + k) % SUBLANES == phase]
                if not taps:
                    continue
                lo, hi = first + taps[0], first + taps[-1]
                n_win = hi - lo + rb
                win_ref[phase, 0:n_win, :] = gl_ref[r0 + lo:r0 + hi + rb, c:c + LANES]
                for k in taps:
                    off = first + k - lo
                    acc = acc + ccw_ref[k:k + 1, c:c + LANES] * win_ref[phase, off:off + rb, :]
            conv_ref[r0:r0 + rb, c:c + LANES] = acc
    cv = conv_ref[...]
    mu = jnp.mean(cv, axis=-1, keepdims=True)
    xc = cv - mu
    var = jnp.mean(xc * xc, axis=-1, keepdims=True)
    cn = xc * lax.rsqrt(var + NORM_EPS) * lng_ref[...] + lnb_ref[...]
    yc_ref[:, D_MODEL:2 * D_MODEL] = (_silu(cn) * _silu(cgate)).astype(yc_ref.dtype)


def _ssd_chunk(L, act_ref, dt, alog_ref, dcol_ref, h_in_ref, h_out_ref, yT_ref):
    a = dt * (-jnp.exp(alog_ref[...]))
    r_i = lax.broadcasted_iota(jnp.int32, (L, L), 0)
    c_i = lax.broadcasted_iota(jnp.int32, (L, L), 1)
    tri = (c_i <= r_i).astype(F32)
    acs = jnp.dot(tri, a, precision=lax.Precision.HIGHEST, preferred_element_type=F32)
    acsT = acs.T
    dtT = dt.T
    last = acsT[:, L - 1:L]
    exp_acsT = jnp.exp(acsT)
    dec_endT = jnp.exp(last - acsT)
    chunk_dec = jnp.exp(last)
    causalT = r_i <= c_i

    xT = act_ref[:, 0:D_MODEL].T
    P = SSD_HEAD_DIM
    hpg = SSD_HEADS // SSD_GROUPS
    gp = hpg * P
    for g in range(SSD_GROUPS):
        b_g = act_ref[:, D_MODEL + g * SSD_STATE:D_MODEL + (g + 1) * SSD_STATE].astype(BF16)
        c_g = act_ref[:, D_MODEL + (SSD_GROUPS + g) * SSD_STATE:D_MODEL + (SSD_GROUPS + g + 1) * SSD_STATE].astype(BF16)
        cbT = lax.dot_general(b_g, c_g, _NT, preferred_element_type=F32)
        h_g = h_in_ref[g * gp:(g + 1) * gp, :]
        y_offT = lax.dot_general(h_g.astype(BF16), c_g, _NT, preferred_element_type=F32)
        st_lhs = []
        for r in range(hpg):
            h = g * hpg + r
            rows = slice(h * P, (h + 1) * P)
            xT_h = xT[rows, :]
            xdtT_h = xT_h * dtT[h:h + 1, :]
            seg = acsT[h:h + 1, :] - acs[:, h:h + 1]
            mT = (jnp.exp(jnp.where(causalT, seg, -jnp.inf)) * cbT).astype(BF16)
            y_dT = jnp.dot(xdtT_h.astype(BF16), mT, preferred_element_type=F32)
            y_oT = y_offT[r * P:(r + 1) * P, :] * exp_acsT[h:h + 1, :]
            yT_ref[rows, :] = y_dT + y_oT + dcol_ref[rows, :] * xT_h
            st_lhs.append((xdtT_h * dec_endT[h:h + 1, :]).astype(BF16))
        st = jnp.dot(jnp.concatenate(st_lhs, axis=0), b_g, preferred_element_type=F32)
        for r in range(hpg):
            h = g * hpg + r
            h_out_ref[h * P:(h + 1) * P, :] = h_g[r * P:(r + 1) * P, :] * chunk_dec[h:h + 1, :] + st[r * P:(r + 1) * P, :]


def _l0_prompt_kernel(x_ref, ng_ref, w_ref, cw_ref, cb_ref, dtb_ref, alog_ref, dcol_ref, sng_ref,
                      ccw_ref, ccb_ref, lng_ref, lnb_ref, wo_ref, h0_ref, sh0_ref, ch0_ref,
                      y_ref, hT_ref, sh_ref, chh_ref,
                      ug_s, ucg_s, uz_s, xbc_s, gl_s, h_s, act_s, yT_s, conv_s, win_s, yc_s):
    i = pl.program_id(0)
    L = CHUNK

    @pl.when(i == 0)
    def _():
        h_s[...] = h0_ref[...]
        xbc_s[0:SSD_HIST_PAD, :] = sh0_ref[...]
        gl_s[0:CONF_HIST_PAD, :] = ch0_ref[...]

    x = x_ref[...]
    xn = _rms_rows(x, ng_ref[...]).astype(BF16)

    def project(col, width, dst_ref, row0=0):
        for c in range(0, width, 512):
            dst_ref[row0:row0 + L, c:c + 512] = jnp.dot(xn, w_ref[:, col + c:col + c + 512], preferred_element_type=F32)

    project(C_XBC, SSD_CONV_DIM, xbc_s, SSD_HIST_PAD)
    dt_raw = jnp.dot(xn, w_ref[:, C_DT:C_DT + LANES], preferred_element_type=F32)
    project(C_GA, 2 * D_MODEL, ug_s)
    project(C_CG, D_MODEL, ucg_s)
    project(C_Z, D_MODEL, uz_s)

    _ssd_conv_silu(xbc_s, cw_ref, cb_ref, L, act_s)
    dt = _softplus(dt_raw + dtb_ref[...])
    _ssd_chunk(L, act_s, dt, alog_ref, dcol_ref, h_s, h_s, yT_s)
    _gated_group_norm(yT_s[...].T, uz_s[...], sng_ref, yc_s)

    gl_s[CONF_HIST_PAD:CONF_HIST_PAD + L, :] = ug_s[:, 0:D_MODEL] * _sigmoid(ug_s[:, D_MODEL:2 * D_MODEL])
    _conformer(gl_s, L, ucg_s[...], ccw_ref, ccb_ref, lng_ref, lnb_ref, conv_s, win_s, yc_s)

    y_ref[...] = x + jnp.dot(yc_s[...], wo_ref[...], preferred_element_type=F32)

    xbc_s[0:SSD_HIST_PAD, :] = xbc_s[L:L + SSD_HIST_PAD, :]
    gl_s[0:CONF_HIST_PAD, :] = gl_s[L:L + CONF_HIST_PAD, :]

    @pl.when(i == pl.num_programs(0) - 1)
    def _():
        hT_ref[...] = h_s[...]
        sh_ref[...] = xbc_s[0:SSD_HIST_PAD, :]
        chh_ref[...] = gl_s[0:CONF_HIST_PAD, :]


def _l0_prompt(x, p, h0, sh0, ch0):
    S = x.shape[0]
    assert S % CHUNK == 0
    nc = S // CHUNK
    row_blk = lambda w: pl.BlockSpec((CHUNK, w), lambda i: (i, 0))
    in_specs = [row_blk(D_MODEL), _const_spec((1, D_MODEL)), _const_spec((D_MODEL, IN0_PAD)),
                _const_spec((SSD_CONV, SSD_CONV_DIM)), _const_spec((1, SSD_CONV_DIM)),
                _const_spec((1, LANES)), _const_spec((1, LANES)), _const_spec((D_MODEL, 1)), _const_spec((1, D_MODEL)),
                _const_spec((CONF_CONV, D_MODEL)), _const_spec((1, D_MODEL)), _const_spec((1, D_MODEL)),
                _const_spec((1, D_MODEL)), _const_spec((2 * D_MODEL, D_MODEL)),
                _const_spec((D_MODEL, SSD_STATE)), _const_spec((SSD_HIST_PAD, SSD_CONV_DIM)),
                _const_spec((CONF_HIST_PAD, D_MODEL))]
    out_shape = (jax.ShapeDtypeStruct((S, D_MODEL), F32), jax.ShapeDtypeStruct((D_MODEL, SSD_STATE), F32),
                 jax.ShapeDtypeStruct((SSD_HIST_PAD, SSD_CONV_DIM), F32), jax.ShapeDtypeStruct((CONF_HIST_PAD, D_MODEL), F32))
    out_specs = (row_blk(D_MODEL), _const_spec((D_MODEL, SSD_STATE)), _const_spec((SSD_HIST_PAD, SSD_CONV_DIM)),
                 _const_spec((CONF_HIST_PAD, D_MODEL)))
    scratch = [pltpu.VMEM((CHUNK, 2 * D_MODEL), F32), pltpu.VMEM((CHUNK, D_MODEL), F32), pltpu.VMEM((CHUNK, D_MODEL), F32),
               pltpu.VMEM((SSD_HIST_PAD + CHUNK, SSD_CONV_DIM), F32),
               pltpu.VMEM((CONF_HIST_PAD + CHUNK, D_MODEL), F32), pltpu.VMEM((D_MODEL, SSD_STATE), F32),
               pltpu.VMEM((CHUNK, SSD_CONV_DIM), F32), pltpu.VMEM((D_MODEL, CHUNK), F32),
               pltpu.VMEM((CHUNK, D_MODEL), F32), _conv_window_scratch(CHUNK), pltpu.VMEM((CHUNK, 2 * D_MODEL), BF16)]
    return pl.pallas_call(
        _l0_prompt_kernel, grid=(nc,), in_specs=in_specs, out_specs=out_specs, out_shape=out_shape,
        scratch_shapes=scratch, name="l0_prompt",
        compiler_params=pltpu.CompilerParams(dimension_semantics=("arbitrary",), vmem_limit_bytes=VMEM_LIMIT),
    )(x, p["ng"], p["w_in"], p["cw"], p["cb"], p["dtb"], p["alog"], p["dcol"], p["sng"],
      p["ccw"], p["ccb"], p["lng"], p["lnb"], p["w_out"], h0, sh0, ch0)


def _norm_proj_kernel(x_ref, g_ref, w_ref, o_ref):
    xn = _rms_rows(x_ref[...], g_ref[...]).astype(BF16)
    o_ref[...] = jnp.dot(xn, w_ref[...], preferred_element_type=F32)


def _norm_proj(x, g, w):
    m, n = x.shape[0], w.shape[1]
    return pl.pallas_call(
        _norm_proj_kernel, grid=(1,),
        in_specs=[_const_spec(x.shape), _const_spec(g.shape), _const_spec(w.shape)],
        out_specs=_const_spec((m, n)), out_shape=jax.ShapeDtypeStruct((m, n), F32), name="norm_proj",
        compiler_params=pltpu.CompilerParams(vmem_limit_bytes=VMEM_LIMIT),
    )(x, g, w)


def _proj_res_kernel(r_ref, a_ref, w_ref, o_ref):
    o_ref[...] = r_ref[...] + jnp.dot(a_ref[...].astype(BF16), w_ref[...], preferred_element_type=F32)


def _proj_res(res, a, w):
    return pl.pallas_call(
        _proj_res_kernel, grid=(1,),
        in_specs=[_const_spec(res.shape), _const_spec(a.shape), _const_spec(w.shape)],
        out_specs=_const_spec(res.shape), out_shape=jax.ShapeDtypeStruct(res.shape, F32), name="proj_res",
        compiler_params=pltpu.CompilerParams(vmem_limit_bytes=VMEM_LIMIT),
    )(res, a, w)


def _l0_sample_kernel(n_tok, u_ref, cw_ref, cb_ref, dtb_ref, alog_ref, dcol_ref, sng_ref,
                      ccw_ref, ccb_ref, lng_ref, lnb_ref, h0_ref, sh0_ref, ch0_ref,
                      yc_ref, hT_ref, sh_ref, chh_ref,
                      xbc_s, gl_s, act_s, yT_s, conv_s, win_s):
    R = 8
    L = SAMPLE_CHUNK
    u = u_ref[0]
    xbc_s[0:SSD_HIST_PAD, :] = sh0_ref[0]
    xbc_s[SSD_HIST_PAD:SSD_HIST_PAD + R, :] = u[:, C_XBC:C_XBC + SSD_CONV_DIM]
    act_s[R:L, :] = jnp.zeros((L - R, SSD_CONV_DIM), F32)
    _ssd_conv_silu(xbc_s, cw_ref, cb_ref, R, act_s.at[0:R])

    dt = _softplus(u[:, C_DT:C_DT + LANES] + dtb_ref[...])
    dt = jnp.where(lax.broadcasted_iota(jnp.int32, (R, LANES), 0) < n_tok, dt, 0.0)
    dt = jnp.concatenate([dt, jnp.zeros((L - R, LANES), F32)], axis=0)
    _ssd_chunk(L, act_s, dt, alog_ref, dcol_ref, h0_ref.at[0], hT_ref.at[0], yT_s)
    y = yT_s[...].T[0:R, :]
    _gated_group_norm(y, u[:, C_Z:C_Z + D_MODEL], sng_ref, yc_ref.at[0])

    gl_s[0:CONF_HIST_PAD, :] = ch0_ref[0]
    gl_s[CONF_HIST_PAD:CONF_HIST_PAD + R, :] = u[:, C_GA:C_GA + D_MODEL] * _sigmoid(u[:, C_GB:C_GB + D_MODEL])
    _conformer(gl_s, R, u[:, C_CG:C_CG + D_MODEL], ccw_ref, ccb_ref, lng_ref, lnb_ref, conv_s, win_s, yc_ref.at[0])

    sh_ref[0] = xbc_s[...]
    chh_ref[0] = gl_s[...]


def _l0_sample(u, p, h0, sh0, ch0, n_tok):
    B = u.shape[0]
    seq_blk = lambda *s: pl.BlockSpec((1,) + s, lambda b: (b,) + (0,) * len(s))
    in_specs = [seq_blk(8, IN0_PAD),
                _const_spec((SSD_CONV, SSD_CONV_DIM)), _const_spec((1, SSD_CONV_DIM)),
                _const_spec((1, LANES)), _const_spec((1, LANES)), _const_spec((D_MODEL, 1)), _const_spec((1, D_MODEL)),
                _const_spec((CONF_CONV, D_MODEL)), _const_spec((1, D_MODEL)), _const_spec((1, D_MODEL)),
                _const_spec((1, D_MODEL)),
                seq_blk(D_MODEL, SSD_STATE), seq_blk(SSD_HIST_PAD, SSD_CONV_DIM), seq_blk(CONF_HIST_PAD, D_MODEL)]
    out_shape = (jax.ShapeDtypeStruct((B, 8, 2 * D_MODEL), F32), jax.ShapeDtypeStruct((B, D_MODEL, SSD_STATE), F32),
                 jax.ShapeDtypeStruct((B, SSD_HIST_PAD + 8, SSD_CONV_DIM), F32),
                 jax.ShapeDtypeStruct((B, CONF_HIST_PAD + 8, D_MODEL), F32))
    out_specs = (seq_blk(8, 2 * D_MODEL), seq_blk(D_MODEL, SSD_STATE), seq_blk(SSD_HIST_PAD + 8, SSD_CONV_DIM),
                 seq_blk(CONF_HIST_PAD + 8, D_MODEL))
    scratch = [pltpu.VMEM((SSD_HIST_PAD + 8, SSD_CONV_DIM), F32), pltpu.VMEM((CONF_HIST_PAD + 8, D_MODEL), F32),
               pltpu.VMEM((SAMPLE_CHUNK, SSD_CONV_DIM), F32),
               pltpu.VMEM((D_MODEL, SAMPLE_CHUNK), F32), pltpu.VMEM((8, D_MODEL), F32), _conv_window_scratch(8)]
    return pl.pallas_call(
        functools.partial(_l0_sample_kernel, n_tok), grid=(B,), in_specs=in_specs, out_specs=out_specs,
        out_shape=out_shape, scratch_shapes=scratch, name="l0_sample",
        compiler_params=pltpu.CompilerParams(dimension_semantics=("arbitrary",), vmem_limit_bytes=VMEM_LIMIT),
    )(u, p["cw"], p["cb"], p["dtb"], p["alog"], p["dcol"], p["sng"], p["ccw"], p["ccb"], p["lng"], p["lnb"],
      h0, sh0, ch0)


def _prep_l0_params(norm_g, w_in, conv_w, conv_b, dt_bias, a_log, d_skip, ssd_norm_g,
                    cconv_w, cconv_b, cln_g, cln_b, w_out):
    o_dt = D_MODEL + SSD_CONV_DIM
    w = jnp.concatenate([w_in[:, :o_dt], w_in[:, o_dt + SSD_HEADS:], w_in[:, o_dt:o_dt + SSD_HEADS],
                         jnp.zeros((D_MODEL, LANES - SSD_HEADS), w_in.dtype)], axis=1)
    lane_pad = lambda v: jnp.pad(v.astype(F32), (0, LANES - SSD_HEADS)).reshape(1, LANES)
    return dict(ng=norm_g.reshape(1, -1), w_in=w.astype(BF16), cw=conv_w, cb=conv_b.reshape(1, -1),
                dtb=lane_pad(dt_bias), alog=lane_pad(a_log),
                dcol=jnp.repeat(d_skip.astype(F32), SSD_HEAD_DIM).reshape(-1, 1), sng=ssd_norm_g.reshape(1, -1),
                ccw=cconv_w, ccb=cconv_b.reshape(1, -1), lng=cln_g.reshape(1, -1), lnb=cln_b.reshape(1, -1),
                w_out=w_out.astype(BF16))


def _hybrid_layer(x_p, x_s, state_ssm, ssd_hist, conf_hist, p):
    S = x_p.shape[1]
    B, T, _ = x_s.shape
    H, P, N = SSD_HEADS, SSD_HEAD_DIM, SSD_STATE
    zeros = lambda *s: jnp.zeros(s, F32)
    yp, h_p, sh_p, ch_p = _l0_prompt(x_p.reshape(S, D_MODEL), p, zeros(D_MODEL, N), zeros(SSD_HIST_PAD, SSD_CONV_DIM),
                                     zeros(CONF_HIST_PAD, D_MODEL))
    xs2 = x_s.reshape(B * T, D_MODEL)
    u_s = _norm_proj(xs2, p["ng"], p["w_in"]).reshape(B, T, IN0_PAD)
    u_s = jnp.pad(u_s, ((0, 0), (0, 8 - T), (0, 0)))
    sh0 = jnp.pad(ssd_hist, ((0, 0), (SSD_HIST_PAD - (SSD_CONV - 1), 0), (0, 0)))
    ch0 = jnp.pad(conf_hist, ((0, 0), (CONF_HIST_PAD - (CONF_CONV - 1), 0), (0, 0)))
    yc_s, h_s, sh_s, ch_s = _l0_sample(u_s, p, state_ssm.reshape(B, H * P, N), sh0, ch0, T)
    ys = _proj_res(xs2, yc_s[:, :T].reshape(B * T, 2 * D_MODEL), p["w_out"])
    return (yp.reshape(1, S, D_MODEL), ys.reshape(B, T, D_MODEL),
            h_p.reshape(1, H, P, N), h_s.reshape(B, H, P, N),
            sh_p[None, SSD_HIST_PAD - (SSD_CONV - 1):], sh_s[:, SSD_HIST_PAD + T - (SSD_CONV - 1):SSD_HIST_PAD + T],
            ch_p[None, CONF_HIST_PAD - (CONF_CONV - 1):], ch_s[:, CONF_HIST_PAD + T - (CONF_CONV - 1):CONF_HIST_PAD + T])


ATT_WIDTH = ATT_HEADS * HEAD_DIM
R_Q, R_K, R_V, R_G, IN1_ROWS = 0, ATT_WIDTH, ATT_WIDTH + KV_WIDTH, ATT_WIDTH + 2 * KV_WIDTH, 2 * ATT_WIDTH + 2 * KV_WIDTH
SCALE = HEAD_DIM ** -0.5
LOG2_E = 1.4426950408889634
V_EXT = HEAD_DIM + 16


def _head_rms_cols(xT, g_col):
    out = []
    for h in range(xT.shape[0] // HEAD_DIM):
        xh = xT[h * HEAD_DIM:(h + 1) * HEAD_DIM, :]
        r = lax.rsqrt(jnp.mean(xh * xh, axis=0, keepdims=True) + NORM_EPS)
        out.append(xh * r * g_col)
    return out


def _l1_proj_kernel(x_ref, ng_ref, wT_ref, qg_ref, kg_ref, qT_ref, gT_ref, kn_ref, vn_ref, kb_ref, vT_ref, km_ref):
    xn = _rms_rows(x_ref[...], ng_ref[...]).astype(BF16)
    uT = lax.dot_general(wT_ref[...], xn, _NT, preferred_element_type=F32)
    for h, qh in enumerate(_head_rms_cols(uT[R_Q:R_K, :], qg_ref[...])):
        qT_ref[h * HEAD_DIM:(h + 1) * HEAD_DIM, :] = (qh * (SCALE * LOG2_E)).astype(BF16)
    kT = jnp.concatenate(_head_rms_cols(uT[R_K:R_V, :], kg_ref[...]), axis=0)
    kn_ref[...] = kT
    k_nat = kT.T
    kb_ref[0] = k_nat.astype(BF16)
    km_ref[0] = jnp.mean(k_nat, axis=0, keepdims=True)
    vT = uT[R_V:R_G, :]
    vn_ref[...] = vT
    for h in range(KV_HEADS):
        vT_ref[0, h * V_EXT:h * V_EXT + HEAD_DIM, :] = vT[h * HEAD_DIM:(h + 1) * HEAD_DIM, :].astype(BF16)
        vT_ref[0, h * V_EXT + HEAD_DIM:(h + 1) * V_EXT, :] = jnp.ones((V_EXT - HEAD_DIM, vT.shape[1]), BF16)
    gT_ref[...] = _silu(uT[R_G:IN1_ROWS, :])


def _l1_proj(x, ng, wT, qg_col, kg_col):
    S = x.shape[0]
    nb = S // CHUNK
    col_blk = lambda r: pl.BlockSpec((r, CHUNK), lambda i: (0, i))
    row_blk = lambda w: pl.BlockSpec((CHUNK, w), lambda i: (i, 0))
    blk3 = lambda a, b: pl.BlockSpec((1, a, b), lambda i: (i, 0, 0))
    out_shape = (jax.ShapeDtypeStruct((ATT_WIDTH, S), BF16), jax.ShapeDtypeStruct((ATT_WIDTH, S), F32),
                 jax.ShapeDtypeStruct((KV_WIDTH, S), F32), jax.ShapeDtypeStruct((KV_WIDTH, S), F32),
                 jax.ShapeDtypeStruct((nb, CHUNK, KV_WIDTH), BF16), jax.ShapeDtypeStruct((nb, KV_HEADS * V_EXT, CHUNK), BF16),
                 jax.ShapeDtypeStruct((nb, 1, KV_WIDTH), F32))
    out_specs = (col_blk(ATT_WIDTH), col_blk(ATT_WIDTH), col_blk(KV_WIDTH), col_blk(KV_WIDTH),
                 blk3(CHUNK, KV_WIDTH), blk3(KV_HEADS * V_EXT, CHUNK), blk3(1, KV_WIDTH))
    return pl.pallas_call(
        _l1_proj_kernel, grid=(nb,),
        in_specs=[row_blk(D_MODEL), _const_spec((1, D_MODEL)), _const_spec((IN1_ROWS, D_MODEL)),
                  _const_spec((HEAD_DIM, 1)), _const_spec((HEAD_DIM, 1))],
        out_specs=out_specs, out_shape=out_shape, name="l1_proj",
        compiler_params=pltpu.CompilerParams(dimension_semantics=("arbitrary",), vmem_limit_bytes=VMEM_LIMIT),
    )(x, ng, wT, qg_col, kg_col)


def _top_k_bias(gate, idx, n_valid_mask, axis):
    n = gate.shape[axis]
    g = jnp.where(n_valid_mask, gate, -jnp.inf)
    sel = jnp.zeros(gate.shape, F32)
    for _ in range(MOBA_TOP_K):
        mx = jnp.max(g, axis=axis, keepdims=True)
        first = jnp.min(jnp.where(g == mx, idx, n), axis=axis, keepdims=True)
        pick = jnp.logical_and(idx == first, mx > -jnp.inf)
        sel = jnp.where(pick, 1.0, sel)
        g = jnp.where(pick, -jnp.inf, g)
    return jnp.where(sel > 0.0, 0.0, MASKED)


def _attn_prompt_kernel(qT_ref, gT_ref, r_ref, kb_ref, vT_ref, km_ref, wo_ref, o_ref, qz_s, bias_s, m_s, a_s,
                        acc_s, og_s, s_s, p_s):
    t = pl.program_id(0)
    L = CHUNK
    nb = km_ref.shape[0]
    gsz = ATT_HEADS // KV_HEADS
    hd = HEAD_DIM

    zeros = jnp.zeros((hd, L), BF16)
    for hq in range(ATT_HEADS):
        q = qT_ref[hq * hd:(hq + 1) * hd, :]
        qz_s[hq] = jnp.concatenate([q, zeros] if (hq // gsz) % 2 == 0 else [zeros, q], axis=0)

    n_i = lax.broadcasted_iota(jnp.int32, (nb, L), 0)
    for hq in range(ATT_HEADS):
        j = (hq // gsz) // 2
        gate = jnp.dot(km_ref[:, j * LANES:(j + 1) * LANES].astype(BF16), qz_s[hq], preferred_element_type=F32)
        bias_s[hq] = _top_k_bias(gate, n_i, n_i < t, 0)

    def scores(hq, n):
        j = (hq // gsz) // 2
        return jnp.dot(kb_ref[n, :, j * LANES:(j + 1) * LANES], qz_s[hq], preferred_element_type=F32)

    def v_rows(hq, n):
        kvh = hq // gsz
        return vT_ref[n, kvh * V_EXT:(kvh + 1) * V_EXT, :]

    causalT = lax.broadcasted_iota(jnp.int32, (L, L), 0) <= lax.broadcasted_iota(jnp.int32, (L, L), 1)

    def stage_scores(n, half, own):
        keys = slice(half * L, (half + 1) * L)
        for hq in range(ATT_HEADS):
            s = scores(hq, n).astype(BF16)
            if own:
                s_s[hq, keys, :] = jnp.where(causalT, s, MASKED)
            else:
                s_s[hq, keys, :] = s + bias_s[hq, pl.ds(n, 1), :].astype(BF16)

    def softmax_and_pv(n, n_blocks, own):
        keys = slice(0, n_blocks * L)
        for hq in range(ATT_HEADS):
            m_new = jnp.max(s_s[hq, keys, :], axis=0, keepdims=True).astype(F32)
            if not own:
                m_old = m_s[hq]
                m_new = jnp.maximum(m_old, m_new)
                a_s[hq] = jnp.exp2(m_old - m_new)
            p_s[hq, keys, :] = jnp.exp2(s_s[hq, keys, :] - m_new.astype(BF16))
            m_s[hq] = m_new
        for hq in range(ATT_HEADS):
            rows = slice(hq * V_EXT, (hq + 1) * V_EXT)
            v = jnp.concatenate([v_rows(hq, n + b) for b in range(n_blocks)], axis=1)
            pv = jnp.dot(v, p_s[hq, keys, :], preferred_element_type=F32)
            acc_s[rows, :] = pv if own else a_s[hq] * acc_s[rows, :] + pv

    stage_scores(t, 0, True)
    softmax_and_pv(t, 1, True)

    def past_pair(i, carry):
        stage_scores(2 * i, 0, False)
        stage_scores(2 * i + 1, 1, False)
        softmax_and_pv(2 * i, 2, False)
        return carry

    lax.fori_loop(0, t // 2, past_pair, 0)

    @pl.when(t % 2 == 1)
    def _():
        stage_scores(t - 1, 0, False)
        softmax_and_pv(t - 1, 1, False)

    for hq in range(ATT_HEADS):
        rows = slice(hq * hd, (hq + 1) * hd)
        og_s[rows, :] = acc_s[hq * V_EXT:hq * V_EXT + hd, :] / acc_s[hq * V_EXT + hd:hq * V_EXT + hd + 1, :] * gT_ref[rows, :]
    og = og_s[...].T.astype(BF16)
    o_ref[...] = r_ref[...] + jnp.dot(og, wo_ref[...], preferred_element_type=F32)


def _attn_prompt(qT, gT, res, kb, vT, km, wo):
    S = res.shape[0]
    nb = S // CHUNK
    col_blk = pl.BlockSpec((ATT_WIDTH, CHUNK), lambda i: (0, i))
    row_blk = pl.BlockSpec((CHUNK, D_MODEL), lambda i: (i, 0))
    scratch = [pltpu.VMEM((ATT_HEADS, 2 * HEAD_DIM, CHUNK), BF16), pltpu.VMEM((ATT_HEADS, nb, CHUNK), F32),
               pltpu.VMEM((ATT_HEADS, 1, CHUNK), F32), pltpu.VMEM((ATT_HEADS, 1, CHUNK), F32),
               pltpu.VMEM((ATT_HEADS * V_EXT, CHUNK), F32), pltpu.VMEM((ATT_WIDTH, CHUNK), F32),
               pltpu.VMEM((ATT_HEADS, 2 * CHUNK, CHUNK), BF16), pltpu.VMEM((ATT_HEADS, 2 * CHUNK, CHUNK), BF16)]
    return pl.pallas_call(
        _attn_prompt_kernel, grid=(nb,),
        in_specs=[col_blk, col_blk, row_blk, _const_spec(kb.shape), _const_spec(vT.shape), _const_spec(km.shape),
                  _const_spec(wo.shape)],
        out_specs=row_blk, out_shape=jax.ShapeDtypeStruct((S, D_MODEL), F32), scratch_shapes=scratch,
        name="attn_prompt",
        compiler_params=pltpu.CompilerParams(dimension_semantics=("arbitrary",), vmem_limit_bytes=VMEM_LIMIT),
    )(qT, gT, res, kb, vT, km, wo)


def _pair_rms(x, g2):
    lo = lax.broadcasted_iota(jnp.int32, (x.shape[0], LANES), 1) < HEAD_DIM
    out = []
    for c in range(0, x.shape[1], LANES):
        xt = x[:, c:c + LANES]
        sq = xt * xt
        s_lo = jnp.sum(jnp.where(lo, sq, 0.0), axis=-1, keepdims=True)
        s_hi = jnp.sum(jnp.where(lo, 0.0, sq), axis=-1, keepdims=True)
        r = jnp.where(lo, lax.rsqrt(s_lo / HEAD_DIM + NORM_EPS), lax.rsqrt(s_hi / HEAD_DIM + NORM_EPS))
        out.append(xt * r * g2)
    return jnp.concatenate(out, axis=-1)


def _l1_proj_nat_kernel(x_ref, ng_ref, w_ref, qg_ref, kg_ref, q_ref, k_ref, v_ref, g_ref):
    xn = _rms_rows(x_ref[...], ng_ref[...]).astype(BF16)
    u = jnp.dot(xn, w_ref[...], preferred_element_type=F32)
    q_ref[...] = _pair_rms(u[:, R_Q:R_K], qg_ref[...]) * SCALE
    k_ref[...] = _pair_rms(u[:, R_K:R_V], kg_ref[...])
    v_ref[...] = u[:, R_V:R_G]
    g_ref[...] = _silu(u[:, R_G:IN1_ROWS])


def _l1_proj_nat(x, ng, w, qg2, kg2):
    m = x.shape[0]
    shapes = ((m, ATT_WIDTH), (m, KV_WIDTH), (m, KV_WIDTH), (m, ATT_WIDTH))
    return pl.pallas_call(
        _l1_proj_nat_kernel, grid=(1,),
        in_specs=[_const_spec(x.shape), _const_spec(ng.shape), _const_spec(w.shape), _const_spec(qg2.shape),
                  _const_spec(kg2.shape)],
        out_specs=tuple(_const_spec(s) for s in shapes),
        out_shape=tuple(jax.ShapeDtypeStruct(s, F32) for s in shapes), name="l1_proj_nat",
        compiler_params=pltpu.CompilerParams(vmem_limit_bytes=VMEM_LIMIT),
    )(x, ng, w, qg2, kg2)


def _attn_sample_kernel(n_tok, bps, pt_ref, q_ref, kn_ref, vn_ref, *refs):
    ppb = CHUNK // PAGE
    npg = bps * ppb
    k_refs, v_refs, o_ref = refs[:npg], refs[npg:2 * npg], refs[2 * npg]
    qh_s, new_s, s_s, m_s, l_s, g_s, o_s, oh_s = refs[2 * npg + 1:]
    step = pl.program_id(1)
    NB = m_s.shape[0] - 1
    R = 8
    gsz = ATT_HEADS // KV_HEADS
    GR = gsz * R
    hd = HEAD_DIM

    @pl.when(step == 0)
    def _():
        q = q_ref[0]
        for hq in range(ATT_HEADS):
            kvh, g = divmod(hq, gsz)
            qh_s[kvh, g * R:(g + 1) * R, :] = q[:, hq * hd:(hq + 1) * hd]

    def partial_softmax(kvh, s, v_h, slot, v_is_transposed):
        m = jnp.max(s, axis=-1, keepdims=True)
        p = jnp.exp(s - m)
        m_s[slot, kvh] = jnp.broadcast_to(m, (GR, hd))
        l_s[slot, kvh] = jnp.broadcast_to(jnp.sum(p, axis=-1, keepdims=True), (GR, hd))
        pv_dims = _NT if v_is_transposed else (((1,), (0,)), ((), ()))
        o_s[slot, kvh] = lax.dot_general(p.astype(BF16), v_h, pv_dims, preferred_element_type=F32)

    def head_T(page_refs, pages, kvh):
        return jnp.concatenate([page_refs[j][0, kvh] for j in pages], axis=-1).astype(BF16)

    for kvh in range(KV_HEADS):
        s_s[kvh] = jnp.dot(qh_s[kvh].astype(BF16), head_T(k_refs, range(npg), kvh), preferred_element_type=F32)
    for kvh in range(KV_HEADS):
        for i in range(bps):
            blk = step * bps + i
            s = s_s[kvh, :, i * CHUNK:(i + 1) * CHUNK]
            g_s[blk, kvh] = jnp.broadcast_to(jnp.sum(s, axis=-1, keepdims=True), (GR, hd))
            partial_softmax(kvh, s, head_T(v_refs, range(i * ppb, (i + 1) * ppb), kvh), blk, True)

    @pl.when(step == pl.num_programs(1) - 1)
    def _():
        new_s[...] = jnp.zeros(new_s.shape, F32)
        kn = kn_ref[0]
        vn = vn_ref[0]
        row_tok = lax.broadcasted_iota(jnp.int32, (GR, LANES), 0) % R
        key = lax.broadcasted_iota(jnp.int32, (GR, LANES), 1)
        own_ok = jnp.logical_and(key <= row_tok, key < n_tok)
        for kvh in range(KV_HEADS):
            new_s[0, kvh, 0:R, :] = kn[:, kvh * hd:(kvh + 1) * hd]
            new_s[1, kvh, 0:R, :] = vn[:, kvh * hd:(kvh + 1) * hd]
            qb = qh_s[kvh].astype(BF16)
            s = lax.dot_general(qb, new_s[0, kvh].astype(BF16), _NT, preferred_element_type=F32)
            partial_softmax(kvh, jnp.where(own_ok, s, MASKED), new_s[1, kvh].astype(BF16), NB, False)

            v1 = v2 = v3 = jnp.full((GR, hd), -jnp.inf, F32)
            i1 = i2 = i3 = jnp.full((GR, hd), -1.0, F32)
            for b in range(NB):
                g = g_s[b, kvh]
                c1, c2, c3 = g > v1, g > v2, g > v3
                v3, i3 = jnp.where(c2, v2, jnp.where(c3, g, v3)), jnp.where(c2, i2, jnp.where(c3, float(b), i3))
                v2, i2 = jnp.where(c1, v1, jnp.where(c2, g, v2)), jnp.where(c1, i1, jnp.where(c2, float(b), i2))
                v1, i1 = jnp.where(c1, g, v1), jnp.where(c1, float(b), i1)
            picked = lambda b: jnp.logical_or(jnp.logical_or(i1 == float(b), i2 == float(b)), i3 == float(b))
            m_tot = m_s[NB, kvh]
            for b in range(NB):
                m_tot = jnp.maximum(m_tot, jnp.where(picked(b), m_s[b, kvh], MASKED))
            w = jnp.exp(m_s[NB, kvh] - m_tot)
            num = w * o_s[NB, kvh]
            den = w * l_s[NB, kvh]
            for b in range(NB):
                w = jnp.where(picked(b), jnp.exp(m_s[b, kvh] - m_tot), 0.0)
                num = num + w * o_s[b, kvh]
                den = den + w * l_s[b, kvh]
            oh_s[kvh] = num / den
        for tile in range(ATT_HEADS // 2):
            pieces = []
            for hq in (2 * tile, 2 * tile + 1):
                kvh, g = divmod(hq, gsz)
                pieces.append(oh_s[kvh, g * R:(g + 1) * R, :])
            o_ref[0, :, tile * LANES:(tile + 1) * LANES] = jnp.concatenate(pieces, axis=-1)


def _attn_sample(page_table, q8, kn8, vn8, cache_kT, cache_vT, n_tok):
    B, n_pages = page_table.shape
    ppb = CHUNK // PAGE
    assert n_pages % ppb == 0, "the new tokens' block is assumed to hold no cached keys"
    nblk = n_pages // ppb
    bps = SAMPLE_BLOCKS_PER_STEP if nblk % SAMPLE_BLOCKS_PER_STEP == 0 else 1
    npg = bps * ppb
    GR = ATT_HEADS // KV_HEADS * 8
    seq_blk = lambda w: pl.BlockSpec((1, 8, w), lambda b, s, pt: (b, 0, 0))
    page_blk = lambda i: pl.BlockSpec((1, KV_HEADS, HEAD_DIM, PAGE), lambda b, s, pt: (pt[b, npg * s + i], 0, 0, 0))
    pages = [page_blk(i) for i in range(npg)]
    part = lambda: pltpu.VMEM((nblk + 1, KV_HEADS, GR, HEAD_DIM), F32)
    grid_spec = pltpu.PrefetchScalarGridSpec(
        num_scalar_prefetch=1, grid=(B, nblk // bps),
        in_specs=[seq_blk(ATT_WIDTH), seq_blk(KV_WIDTH), seq_blk(KV_WIDTH)] + pages + pages,
        out_specs=seq_blk(ATT_WIDTH),
        scratch_shapes=[pltpu.VMEM((KV_HEADS, GR, HEAD_DIM), F32), pltpu.VMEM((2, KV_HEADS, LANES, HEAD_DIM), F32),
                        pltpu.VMEM((KV_HEADS, GR, bps * CHUNK), F32), part(), part(), part(), part(),
                        pltpu.VMEM((KV_HEADS, GR, HEAD_DIM), F32)])
    return pl.pallas_call(
        functools.partial(_attn_sample_kernel, n_tok, bps), grid_spec=grid_spec,
        out_shape=jax.ShapeDtypeStruct((B, 8, ATT_WIDTH), F32), name="attn_sample",
        compiler_params=pltpu.CompilerParams(dimension_semantics=("arbitrary", "arbitrary"), vmem_limit_bytes=VMEM_LIMIT),
    )(page_table, q8, kn8, vn8, *([cache_kT] * npg), *([cache_vT] * npg))


def _gated_proj_res_kernel(r_ref, a_ref, g_ref, w_ref, o_ref):
    og = (a_ref[...] * g_ref[...]).astype(BF16)
    o_ref[...] = r_ref[...] + jnp.dot(og, w_ref[...], preferred_element_type=F32)


def _gated_proj_res(res, a, g, w):
    return pl.pallas_call(
        _gated_proj_res_kernel, grid=(1,),
        in_specs=[_const_spec(res.shape), _const_spec(a.shape), _const_spec(g.shape), _const_spec(w.shape)],
        out_specs=_const_spec(res.shape), out_shape=jax.ShapeDtypeStruct(res.shape, F32), name="gated_proj_res",
        compiler_params=pltpu.CompilerParams(vmem_limit_bytes=VMEM_LIMIT),
    )(res, a, g, w)


def _attention_layer(yp, ys, cache_k, cache_v, page_table, norm_g, w_in, qn_g, kn_g, w_out):
    S = yp.shape[1]
    B, T, _ = ys.shape
    ng = norm_g.reshape(1, -1)
    wo = w_out.astype(BF16)
    qT, gT, k_p, v_p, kb, vT, km = _l1_proj(yp.reshape(S, D_MODEL), ng, w_in.T.astype(BF16),
                                            qn_g.reshape(-1, 1), kn_g.reshape(-1, 1))
    yp2 = _attn_prompt(qT, gT, yp.reshape(S, D_MODEL), kb, vT, km.reshape(-1, KV_WIDTH), wo)

    ys2 = ys.reshape(B * T, D_MODEL)
    pair = lambda g: jnp.concatenate([g, g]).reshape(1, LANES)
    q_s, k_s, v_s, g_s = _l1_proj_nat(ys2, ng, w_in.astype(BF16), pair(qn_g), pair(kn_g))
    pad_tok = lambda a: jnp.pad(a.reshape(B, T, -1), ((0, 0), (0, 8 - T), (0, 0)))
    paged = lambda c: jnp.transpose(c, (0, 2, 3, 1))
    o_s = _attn_sample(page_table, pad_tok(q_s), pad_tok(k_s), pad_tok(v_s), paged(cache_k), paged(cache_v), T)
    ys3 = _gated_proj_res(ys2, o_s[:, :T].reshape(B * T, ATT_WIDTH), g_s, wo)
    tok_major = lambda aT: jnp.transpose(aT.reshape(KV_HEADS, HEAD_DIM, S), (2, 0, 1))[None]
    return (yp2.reshape(1, S, D_MODEL), ys3.reshape(B, T, D_MODEL), tok_major(k_p), tok_major(v_p),
            k_s.reshape(B, T, KV_HEADS, HEAD_DIM), v_s.reshape(B, T, KV_HEADS, HEAD_DIM))


def kernel(x_prompt, x_sample, state_ssm, state_ssd_conv, state_conf_conv, cache_k, cache_v, page_table, norm0_g, w_in0, ssd_conv_w, ssd_conv_b, ssd_dt_bias, ssd_a_log, ssd_d, ssd_norm_g, conf_conv_w, conf_conv_b, conf_ln_g, conf_ln_b, w_out0, norm1_g, w_in1, q_norm_g, k_norm_g, w_out1):
    p0 = _prep_l0_params(norm0_g[0], w_in0[0], ssd_conv_w[0], ssd_conv_b[0], ssd_dt_bias[0], ssd_a_log[0], ssd_d[0],
                         ssd_norm_g[0], conf_conv_w[0], conf_conv_b[0], conf_ln_g[0], conf_ln_b[0], w_out0[0])
    yp, ys, h_p, h_s, sh_p, sh_s, ch_p, ch_s = _hybrid_layer(
        x_prompt, x_sample, state_ssm[0], state_ssd_conv[0], state_conf_conv[0], p0)
    yp, ys, k_p, v_p, k_s, v_s = _attention_layer(yp, ys, cache_k[0], cache_v[0], page_table, norm1_g[0], w_in1[0],
                                                  q_norm_g[0], k_norm_g[0], w_out1[0])
    return (yp, ys, h_p[None], h_s[None], sh_p[None], sh_s[None], ch_p[None], ch_s[None],
            k_p[None], v_p[None], k_s[None], v_s[None])
```

```python
import functools

import jax
import jax.numpy as jnp
from jax import lax
from jax.experimental import pallas as pl
from jax.experimental.pallas import tpu as pltpu

F32 = jnp.float32
BF16 = jnp.bfloat16
NORM_EPS = 1e-6
MASKED = -1e30
LANES = 128
SUBLANES = 8
CHUNK = 256
D_MODEL = 1024
SSD_HEADS = 16
SSD_HEAD_DIM = 64
SSD_GROUPS = 4
SSD_STATE = 128
SSD_CONV = 4
SSD_CONV_DIM = 2048
CONF_CONV = 31
ATT_HEADS = 16
KV_HEADS = 4
HEAD_DIM = 64
KV_WIDTH = KV_HEADS * HEAD_DIM
MOBA_TOP_K = 3
PAGE = 128
C_Z, C_XBC, C_GA, C_GB, C_CG, C_DT, IN0_PAD = 0, 1024, 3072, 4096, 5120, 6144, 6272
SSD_HIST_PAD = 8
CONF_HIST_PAD = 32
SAMPLE_CHUNK = 128
SAMPLE_BLOCKS_PER_STEP = 8
VMEM_LIMIT = 56 * 1024 * 1024

_NT = (((1,), (1,)), ((), ()))


def _sigmoid(x):
    return 1.0 / (1.0 + jnp.exp(-x))


def _silu(x):
    return x * _sigmoid(x)


def _softplus(x):
    return jnp.maximum(x, 0.0) + jnp.log1p(jnp.exp(-jnp.abs(x)))


def _rms_rows(x, g):
    ms = jnp.mean(x * x, axis=-1, keepdims=True)
    return x * lax.rsqrt(ms + NORM_EPS) * g


def _const_spec(shape):
    nd = len(shape)
    return pl.BlockSpec(shape, lambda *_: (0,) * nd)


def _ssd_conv_silu(buf_ref, cw_ref, cb_ref, rows, out_ref):
    for c in range(0, SSD_CONV_DIM, 512):
        acc = cb_ref[:, c:c + 512]
        for k in range(SSD_CONV):
            o = SSD_HIST_PAD - (SSD_CONV - 1) + k
            acc = acc + cw_ref[k:k + 1, c:c + 512] * buf_ref[o:o + rows, c:c + 512]
        out_ref[:, c:c + 512] = _silu(acc)


def _gated_group_norm(y, z, sng_ref, yc_ref):
    y = y * _silu(z)
    gw = D_MODEL // SSD_GROUPS
    for g in range(SSD_GROUPS):
        yg = y[:, g * gw:(g + 1) * gw]
        r = lax.rsqrt(jnp.mean(yg * yg, axis=-1, keepdims=True) + NORM_EPS)
        yc_ref[:, g * gw:(g + 1) * gw] = (yg * r * sng_ref[:, g * gw:(g + 1) * gw]).astype(yc_ref.dtype)


CONV_WIN_EXTRA = (CONF_CONV - 1) // SUBLANES * SUBLANES


def _conv_window_scratch(rows):
    return pltpu.VMEM((SUBLANES, min(rows, 128) + CONV_WIN_EXTRA, LANES), F32)


def _conformer(gl_ref, rows, cgate, ccw_ref, ccb_ref, lng_ref, lnb_ref, conv_ref, win_ref, yc_ref):
    first = CONF_HIST_PAD - (CONF_CONV - 1)
    rb = min(rows, 128)
    for c in range(0, D_MODEL, LANES):
        for r0 in range(0, rows, rb):
            acc = jnp.broadcast_to(ccb_ref[:, c:c + LANES], (rb, LANES))
            for phase in range(SUBLANES):
                taps = [k for k in range(CONF_CONV) if (first + k) % SUBLANES == phase]
                if not taps:
                    continue
                lo, hi = first + taps[0], first + taps[-1]
                n_win = hi - lo + rb
                win_ref[phase, 0:n_win, :] = gl_ref[r0 + lo:r0 + hi + rb, c:c + LANES]
                for k in taps:
                    off = first + k - lo
                    acc = acc + ccw_ref[k:k + 1, c:c + LANES] * win_ref[phase, off:off + rb, :]
            conv_ref[r0:r0 + rb, c:c + LANES] = acc
    cv = conv_ref[...]
    mu = jnp.mean(cv, axis=-1, keepdims=True)
    xc = cv - mu
    var = jnp.mean(xc * xc, axis=-1, keepdims=True)
    cn = xc * lax.rsqrt(var + NORM_EPS) * lng_ref[...] + lnb_ref[...]
    yc_ref[:, D_MODEL:2 * D_MODEL] = (_silu(cn) * _silu(cgate)).astype(yc_ref.dtype)


def _ssd_chunk(L, act_ref, dt, alog_ref, dcol_ref, h_in_ref, h_out_ref, yT_ref):
    a = dt * (-jnp.exp(alog_ref[...]))
    r_i = lax.broadcasted_iota(jnp.int32, (L, L), 0)
    c_i = lax.broadcasted_iota(jnp.int32, (L, L), 1)
    tri = (c_i <= r_i).astype(F32)
    acs = jnp.dot(tri, a, precision=lax.Precision.HIGHEST, preferred_element_type=F32)
    acsT = acs.T
    dtT = dt.T
    last = acsT[:, L - 1:L]
    exp_acsT = jnp.exp(acsT)
    dec_endT = jnp.exp(last - acsT)
    chunk_dec = jnp.exp(last)
    causalT = r_i <= c_i

    xT = act_ref[:, 0:D_MODEL].T
    P = SSD_HEAD_DIM
    hpg = SSD_HEADS // SSD_GROUPS
    gp = hpg * P
    for g in range(SSD_GROUPS):
        b_g = act_ref[:, D_MODEL + g * SSD_STATE:D_MODEL + (g + 1) * SSD_STATE].astype(BF16)
        c_g = act_ref[:, D_MODEL + (SSD_GROUPS + g) * SSD_STATE:D_MODEL + (SSD_GROUPS + g + 1) * SSD_STATE].astype(BF16)
        cbT = lax.dot_general(b_g, c_g, _NT, preferred_element_type=F32)
        h_g = h_in_ref[g * gp:(g + 1) * gp, :]
        y_offT = lax.dot_general(h_g.astype(BF16), c_g, _NT, preferred_element_type=F32)
        st_lhs = []
        for r in range(hpg):
            h = g * hpg + r
            rows = slice(h * P, (h + 1) * P)
            xT_h = xT[rows, :]
            xdtT_h = xT_h * dtT[h:h + 1, :]
            seg = acsT[h:h + 1, :] - acs[:, h:h + 1]
            mT = (jnp.exp(jnp.where(causalT, seg, -jnp.inf)) * cbT).astype(BF16)
            y_dT = jnp.dot(xdtT_h.astype(BF16), mT, preferred_element_type=F32)
            y_oT = y_offT[r * P:(r + 1) * P, :] * exp_acsT[h:h + 1, :]
            yT_ref[rows, :] = y_dT + y_oT + dcol_ref[rows, :] * xT_h
            st_lhs.append((xdtT_h * dec_endT[h:h + 1, :]).astype(BF16))
        st = jnp.dot(jnp.concatenate(st_lhs, axis=0), b_g, preferred_element_type=F32)
        for r in range(hpg):
            h = g * hpg + r
            h_out_ref[h * P:(h + 1) * P, :] = h_g[r * P:(r + 1) * P, :] * chunk_dec[h:h + 1, :] + st[r * P:(r + 1) * P, :]


def _l0_prompt_kernel(x_ref, ng_ref, w_ref, cw_ref, cb_ref, dtb_ref, alog_ref, dcol_ref, sng_ref,
                      ccw_ref, ccb_ref, lng_ref, lnb_ref, wo_ref, h0_ref, sh0_ref, ch0_ref,
                      y_ref, hT_ref, sh_ref, chh_ref,
                      ug_s, ucg_s, uz_s, xbc_s, gl_s, h_s, act_s, yT_s, conv_s, win_s, yc_s):
    i = pl.program_id(0)
    L = CHUNK

    @pl.when(i == 0)
    def _():
        h_s[...] = h0_ref[...]
        xbc_s[0:SSD_HIST_PAD, :] = sh0_ref[...]
        gl_s[0:CONF_HIST_PAD, :] = ch0_ref[...]

    x = x_ref[...]
    xn = _rms_rows(x, ng_ref[...]).astype(BF16)

    def project(col, width, dst_ref, row0=0):
        for c in range(0, width, 512):
            dst_ref[row0:row0 + L, c:c + 512] = jnp.dot(xn, w_ref[:, col + c:col + c + 512], preferred_element_type=F32)

    project(C_XBC, SSD_CONV_DIM, xbc_s, SSD_HIST_PAD)
    dt_raw = jnp.dot(xn, w_ref[:, C_DT:C_DT + LANES], preferred_element_type=F32)
    project(C_GA, 2 * D_MODEL, ug_s)
    project(C_CG, D_MODEL, ucg_s)
    project(C_Z, D_MODEL, uz_s)

    _ssd_conv_silu(xbc_s, cw_ref, cb_ref, L, act_s)
    dt = _softplus(dt_raw + dtb_ref[...])
    _ssd_chunk(L, act_s, dt, alog_ref, dcol_ref, h_s, h_s, yT_s)
    _gated_group_norm(yT_s[...].T, uz_s[...], sng_ref, yc_s)

    gl_s[CONF_HIST_PAD:CONF_HIST_PAD + L, :] = ug_s[:, 0:D_MODEL] * _sigmoid(ug_s[:, D_MODEL:2 * D_MODEL])
    _conformer(gl_s, L, ucg_s[...], ccw_ref, ccb_ref, lng_ref, lnb_ref, conv_s, win_s, yc_s)

    y_ref[...] = x + jnp.dot(yc_s[...], wo_ref[...], preferred_element_type=F32)

    xbc_s[0:SSD_HIST_PAD, :] = xbc_s[L:L + SSD_HIST_PAD, :]
    gl_s[0:CONF_HIST_PAD, :] = gl_s[L:L + CONF_HIST_PAD, :]

    @pl.when(i == pl.num_programs(0) - 1)
    def _():
        hT_ref[...] = h_s[...]
        sh_ref[...] = xbc_s[0:SSD_HIST_PAD, :]
        chh_ref[...] = gl_s[0:CONF_HIST_PAD, :]


def _l0_prompt(x, p, h0, sh0, ch0):
    S = x.shape[0]
    assert S % CHUNK == 0
    nc = S // CHUNK
    row_blk = lambda w: pl.BlockSpec((CHUNK, w), lambda i: (i, 0))
    in_specs = [row_blk(D_MODEL), _const_spec((1, D_MODEL)), _const_spec((D_MODEL, IN0_PAD)),
                _const_spec((SSD_CONV, SSD_CONV_DIM)), _const_spec((1, SSD_CONV_DIM)),
                _const_spec((1, LANES)), _const_spec((1, LANES)), _const_spec((D_MODEL, 1)), _const_spec((1, D_MODEL)),
                _const_spec((CONF_CONV, D_MODEL)), _const_spec((1, D_MODEL)), _const_spec((1, D_MODEL)),
                _const_spec((1, D_MODEL)), _const_spec((2 * D_MODEL, D_MODEL)),
                _const_spec((D_MODEL, SSD_STATE)), _const_spec((SSD_HIST_PAD, SSD_CONV_DIM)),
                _const_spec((CONF_HIST_PAD, D_MODEL))]
    out_shape = (jax.ShapeDtypeStruct((S, D_MODEL), F32), jax.ShapeDtypeStruct((D_MODEL, SSD_STATE), F32),
                 jax.ShapeDtypeStruct((SSD_HIST_PAD, SSD_CONV_DIM), F32), jax.ShapeDtypeStruct((CONF_HIST_PAD, D_MODEL), F32))
    out_specs = (row_blk(D_MODEL), _const_spec((D_MODEL, SSD_STATE)), _const_spec((SSD_HIST_PAD, SSD_CONV_DIM)),
                 _const_spec((CONF_HIST_PAD, D_MODEL)))
    scratch = [pltpu.VMEM((CHUNK, 2 * D_MODEL), F32), pltpu.VMEM((CHUNK, D_MODEL), F32), pltpu.VMEM((CHUNK, D_MODEL), F32),
               pltpu.VMEM((SSD_HIST_PAD + CHUNK, SSD_CONV_DIM), F32),
               pltpu.VMEM((CONF_HIST_PAD + CHUNK, D_MODEL), F32), pltpu.VMEM((D_MODEL, SSD_STATE), F32),
               pltpu.VMEM((CHUNK, SSD_CONV_DIM), F32), pltpu.VMEM((D_MODEL, CHUNK), F32),
               pltpu.VMEM((CHUNK, D_MODEL), F32), _conv_window_scratch(CHUNK), pltpu.VMEM((CHUNK, 2 * D_MODEL), BF16)]
    return pl.pallas_call(
        _l0_prompt_kernel, grid=(nc,), in_specs=in_specs, out_specs=out_specs, out_shape=out_shape,
        scratch_shapes=scratch, name="l0_prompt",
        compiler_params=pltpu.CompilerParams(dimension_semantics=("arbitrary",), vmem_limit_bytes=VMEM_LIMIT),
    )(x, p["ng"], p["w_in"], p["cw"], p["cb"], p["dtb"], p["alog"], p["dcol"], p["sng"],
      p["ccw"], p["ccb"], p["lng"], p["lnb"], p["w_out"], h0, sh0, ch0)


def _norm_proj_kernel(x_ref, g_ref, w_ref, o_ref):
    xn = _rms_rows(x_ref[...], g_ref[...]).astype(BF16)
    o_ref[...] = jnp.dot(xn, w_ref[...], preferred_element_type=F32)


def _norm_proj(x, g, w):
    m, n = x.shape[0], w.shape[1]
    return pl.pallas_call(
        _norm_proj_kernel, grid=(1,),
        in_specs=[_const_spec(x.shape), _const_spec(g.shape), _const_spec(w.shape)],
        out_specs=_const_spec((m, n)), out_shape=jax.ShapeDtypeStruct((m, n), F32), name="norm_proj",
        compiler_params=pltpu.CompilerParams(vmem_limit_bytes=VMEM_LIMIT),
    )(x, g, w)


def _proj_res_kernel(r_ref, a_ref, w_ref, o_ref):
    o_ref[...] = r_ref[...] + jnp.dot(a_ref[...].astype(BF16), w_ref[...], preferred_element_type=F32)


def _proj_res(res, a, w):
    return pl.pallas_call(
        _proj_res_kernel, grid=(1,),
        in_specs=[_const_spec(res.shape), _const_spec(a.shape), _const_spec(w.shape)],
        out_specs=_const_spec(res.shape), out_shape=jax.ShapeDtypeStruct(res.shape, F32), name="proj_res",
        compiler_params=pltpu.CompilerParams(vmem_limit_bytes=VMEM_LIMIT),
    )(res, a, w)


def _l0_sample_kernel(n_tok, u_ref, cw_ref, cb_ref, dtb_ref, alog_ref, dcol_ref, sng_ref,
                      ccw_ref, ccb_ref, lng_ref, lnb_ref, h0_ref, sh0_ref, ch0_ref,
                      yc_ref, hT_ref, sh_ref, chh_ref,
                      xbc_s, gl_s, act_s, yT_s, conv_s, win_s):
    R = 8
    L = SAMPLE_CHUNK
    u = u_ref[0]
    xbc_s[0:SSD_HIST_PAD, :] = sh0_ref[0]
    xbc_s[SSD_HIST_PAD:SSD_HIST_PAD + R, :] = u[:, C_XBC:C_XBC + SSD_CONV_DIM]
    act_s[R:L, :] = jnp.zeros((L - R, SSD_CONV_DIM), F32)
    _ssd_conv_silu(xbc_s, cw_ref, cb_ref, R, act_s.at[0:R])

    dt = _softplus(u[:, C_DT:C_DT + LANES] + dtb_ref[...])
    dt = jnp.where(lax.broadcasted_iota(jnp.int32, (R, LANES), 0) < n_tok, dt, 0.0)
    dt = jnp.concatenate([dt, jnp.zeros((L - R, LANES), F32)], axis=0)
    _ssd_chunk(L, act_s, dt, alog_ref, dcol_ref, h0_ref.at[0], hT_ref.at[0], yT_s)
    y = yT_s[...].T[0:R, :]
    _gated_group_norm(y, u[:, C_Z:C_Z + D_MODEL], sng_ref, yc_ref.at[0])

    gl_s[0:CONF_HIST_PAD, :] = ch0_ref[0]
    gl_s[CONF_HIST_PAD:CONF_HIST_PAD + R, :] = u[:, C_GA:C_GA + D_MODEL] * _sigmoid(u[:, C_GB:C_GB + D_MODEL])
    _conformer(gl_s, R, u[:, C_CG:C_CG + D_MODEL], ccw_ref, ccb_ref, lng_ref, lnb_ref, conv_s, win_s, yc_ref.at[0])

    sh_ref[0] = xbc_s[...]
    chh_ref[0] = gl_s[...]


def _l0_sample(u, p, h0, sh0, ch0, n_tok):
    B = u.shape[0]
    seq_blk = lambda *s: pl.BlockSpec((1,) + s, lambda b: (b,) + (0,) * len(s))
    in_specs = [seq_blk(8, IN0_PAD),
                _const_spec((SSD_CONV, SSD_CONV_DIM)), _const_spec((1, SSD_CONV_DIM)),
                _const_spec((1, LANES)), _const_spec((1, LANES)), _const_spec((D_MODEL, 1)), _const_spec((1, D_MODEL)),
                _const_spec((CONF_CONV, D_MODEL)), _const_spec((1, D_MODEL)), _const_spec((1, D_MODEL)),
                _const_spec((1, D_MODEL)),
                seq_blk(D_MODEL, SSD_STATE), seq_blk(SSD_HIST_PAD, SSD_CONV_DIM), seq_blk(CONF_HIST_PAD, D_MODEL)]
    out_shape = (jax.ShapeDtypeStruct((B, 8, 2 * D_MODEL), F32), jax.ShapeDtypeStruct((B, D_MODEL, SSD_STATE), F32),
                 jax.ShapeDtypeStruct((B, SSD_HIST_PAD + 8, SSD_CONV_DIM), F32),
                 jax.ShapeDtypeStruct((B, CONF_HIST_PAD + 8, D_MODEL), F32))
    out_specs = (seq_blk(8, 2 * D_MODEL), seq_blk(D_MODEL, SSD_STATE), seq_blk(SSD_HIST_PAD + 8, SSD_CONV_DIM),
                 seq_blk(CONF_HIST_PAD + 8, D_MODEL))
    scratch = [pltpu.VMEM((SSD_HIST_PAD + 8, SSD_CONV_DIM), F32), pltpu.VMEM((CONF_HIST_PAD + 8, D_MODEL), F32),
               pltpu.VMEM((SAMPLE_CHUNK, SSD_CONV_DIM), F32),
               pltpu.VMEM((D_MODEL, SAMPLE_CHUNK), F32), pltpu.VMEM((8, D_MODEL), F32), _conv_window_scratch(8)]
    return pl.pallas_call(
        functools.partial(_l0_sample_kernel, n_tok), grid=(B,), in_specs=in_specs, out_specs=out_specs,
        out_shape=out_shape, scratch_shapes=scratch, name="l0_sample",
        compiler_params=pltpu.CompilerParams(dimension_semantics=("arbitrary",), vmem_limit_bytes=VMEM_LIMIT),
    )(u, p["cw"], p["cb"], p["dtb"], p["alog"], p["dcol"], p["sng"], p["ccw"], p["ccb"], p["lng"], p["lnb"],
      h0, sh0, ch0)


def _prep_l0_params(norm_g, w_in, conv_w, conv_b, dt_bias, a_log, d_skip, ssd_norm_g,
                    cconv_w, cconv_b, cln_g, cln_b, w_out):
    o_dt = D_MODEL + SSD_CONV_DIM
    w = jnp.concatenate([w_in[:, :o_dt], w_in[:, o_dt + SSD_HEADS:], w_in[:, o_dt:o_dt + SSD_HEADS],
                         jnp.zeros((D_MODEL, LANES - SSD_HEADS), w_in.dtype)], axis=1)
    lane_pad = lambda v: jnp.pad(v.astype(F32), (0, LANES - SSD_HEADS)).reshape(1, LANES)
    return dict(ng=norm_g.reshape(1, -1), w_in=w.astype(BF16), cw=conv_w, cb=conv_b.reshape(1, -1),
                dtb=lane_pad(dt_bias), alog=lane_pad(a_log),
                dcol=jnp.repeat(d_skip.astype(F32), SSD_HEAD_DIM).reshape(-1, 1), sng=ssd_norm_g.reshape(1, -1),
                ccw=cconv_w, ccb=cconv_b.reshape(1, -1), lng=cln_g.reshape(1, -1), lnb=cln_b.reshape(1, -1),
                w_out=w_out.astype(BF16))


def _hybrid_layer(x_p, x_s, state_ssm, ssd_hist, conf_hist, p):
    S = x_p.shape[1]
    B, T, _ = x_s.shape
    H, P, N = SSD_HEADS, SSD_HEAD_DIM, SSD_STATE
    zeros = lambda *s: jnp.zeros(s, F32)
    yp, h_p, sh_p, ch_p = _l0_prompt(x_p.reshape(S, D_MODEL), p, zeros(D_MODEL, N), zeros(SSD_HIST_PAD, SSD_CONV_DIM),
                                     zeros(CONF_HIST_PAD, D_MODEL))
    xs2 = x_s.reshape(B * T, D_MODEL)
    u_s = _norm_proj(xs2, p["ng"], p["w_in"]).reshape(B, T, IN0_PAD)
    u_s = jnp.pad(u_s, ((0, 0), (0, 8 - T), (0, 0)))
    sh0 = jnp.pad(ssd_hist, ((0, 0), (SSD_HIST_PAD - (SSD_CONV - 1), 0), (0, 0)))
    ch0 = jnp.pad(conf_hist, ((0, 0), (CONF_HIST_PAD - (CONF_CONV - 1), 0), (0, 0)))
    yc_s, h_s, sh_s, ch_s = _l0_sample(u_s, p, state_ssm.reshape(B, H * P, N), sh0, ch0, T)
    ys = _proj_res(xs2, yc_s[:, :T].reshape(B * T, 2 * D_MODEL), p["w_out"])
    return (yp.reshape(1, S, D_MODEL), ys.reshape(B, T, D_MODEL),
            h_p.reshape(1, H, P, N), h_s.reshape(B, H, P, N),
            sh_p[None, SSD_HIST_PAD - (SSD_CONV - 1):], sh_s[:, SSD_HIST_PAD + T - (SSD_CONV - 1):SSD_HIST_PAD + T],
            ch_p[None, CONF_HIST_PAD - (CONF_CONV - 1):], ch_s[:, CONF_HIST_PAD + T - (CONF_CONV - 1):CONF_HIST_PAD + T])


ATT_WIDTH = ATT_HEADS * HEAD_DIM
R_Q, R_K, R_V, R_G, IN1_ROWS = 0, ATT_WIDTH, ATT_WIDTH + KV_WIDTH, ATT_WIDTH + 2 * KV_WIDTH, 2 * ATT_WIDTH + 2 * KV_WIDTH
SCALE = HEAD_DIM ** -0.5
LOG2_E = 1.4426950408889634
V_EXT = HEAD_DIM + 16


def _head_rms_cols(xT, g_col):
    out = []
    for h in range(xT.shape[0] // HEAD_DIM):
        xh = xT[h * HEAD_DIM:(h + 1) * HEAD_DIM, :]
        r = lax.rsqrt(jnp.mean(xh * xh, axis=0, keepdims=True) + NORM_EPS)
        out.append(xh * r * g_col)
    return out


def _l1_proj_kernel(x_ref, ng_ref, wT_ref, qg_ref, kg_ref, qT_ref, gT_ref, kn_ref, vn_ref, kb_ref, vT_ref, km_ref):
    xn = _rms_rows(x_ref[...], ng_ref[...]).astype(BF16)
    uT = lax.dot_general(wT_ref[...], xn, _NT, preferred_element_type=F32)
    for h, qh in enumerate(_head_rms_cols(uT[R_Q:R_K, :], qg_ref[...])):
        qT_ref[h * HEAD_DIM:(h + 1) * HEAD_DIM, :] = (qh * (SCALE * LOG2_E)).astype(BF16)
    kT = jnp.concatenate(_head_rms_cols(uT[R_K:R_V, :], kg_ref[...]), axis=0)
    kn_ref[...] = kT
    k_nat = kT.T
    kb_ref[0] = k_nat.astype(BF16)
    km_ref[0] = jnp.mean(k_nat, axis=0, keepdims=True)
    vT = uT[R_V:R_G, :]
    vn_ref[...] = vT
    for h in range(KV_HEADS):
        vT_ref[0, h * V_EXT:h * V_EXT + HEAD_DIM, :] = vT[h * HEAD_DIM:(h + 1) * HEAD_DIM, :].astype(BF16)
        vT_ref[0, h * V_EXT + HEAD_DIM:(h + 1) * V_EXT, :] = jnp.ones((V_EXT - HEAD_DIM, vT.shape[1]), BF16)
    gT_ref[...] = _silu(uT[R_G:IN1_ROWS, :])


def _l1_proj(x, ng, wT, qg_col, kg_col):
    S = x.shape[0]
    nb = S // CHUNK
    col_blk = lambda r: pl.BlockSpec((r, CHUNK), lambda i: (0, i))
    row_blk = lambda w: pl.BlockSpec((CHUNK, w), lambda i: (i, 0))
    blk3 = lambda a, b: pl.BlockSpec((1, a, b), lambda i: (i, 0, 0))
    out_shape = (jax.ShapeDtypeStruct((ATT_WIDTH, S), BF16), jax.ShapeDtypeStruct((ATT_WIDTH, S), F32),
                 jax.ShapeDtypeStruct((KV_WIDTH, S), F32), jax.ShapeDtypeStruct((KV_WIDTH, S), F32),
                 jax.ShapeDtypeStruct((nb, CHUNK, KV_WIDTH), BF16), jax.ShapeDtypeStruct((nb, KV_HEADS * V_EXT, CHUNK), BF16),
                 jax.ShapeDtypeStruct((nb, 1, KV_WIDTH), F32))
    out_specs = (col_blk(ATT_WIDTH), col_blk(ATT_WIDTH), col_blk(KV_WIDTH), col_blk(KV_WIDTH),
                 blk3(CHUNK, KV_WIDTH), blk3(KV_HEADS * V_EXT, CHUNK), blk3(1, KV_WIDTH))
    return pl.pallas_call(
        _l1_proj_kernel, grid=(nb,),
        in_specs=[row_blk(D_MODEL), _const_spec((1, D_MODEL)), _const_spec((IN1_ROWS, D_MODEL)),
                  _const_spec((HEAD_DIM, 1)), _const_spec((HEAD_DIM, 1))],
        out_specs=out_specs, out_shape=out_shape, name="l1_proj",
        compiler_params=pltpu.CompilerParams(dimension_semantics=("arbitrary",), vmem_limit_bytes=VMEM_LIMIT),
    )(x, ng, wT, qg_col, kg_col)


def _top_k_bias(gate, idx, n_valid_mask, axis):
    n = gate.shape[axis]
    g = jnp.where(n_valid_mask, gate, -jnp.inf)
    sel = jnp.zeros(gate.shape, F32)
    for _ in range(MOBA_TOP_K):
        mx = jnp.max(g, axis=axis, keepdims=True)
        first = jnp.min(jnp.where(g == mx, idx, n), axis=axis, keepdims=True)
        pick = jnp.logical_and(idx == first, mx > -jnp.inf)
        sel = jnp.where(pick, 1.0, sel)
        g = jnp.where(pick, -jnp.inf, g)
    return jnp.where(sel > 0.0, 0.0, MASKED)


def _attn_prompt_kernel(qT_ref, gT_ref, r_ref, kb_ref, vT_ref, km_ref, wo_ref, o_ref, qz_s, bias_s, m_s, a_s,
                        acc_s, og_s, s_s, p_s):
    t = pl.program_id(0)
    L = CHUNK
    nb = km_ref.shape[0]
    gsz = ATT_HEADS // KV_HEADS
    hd = HEAD_DIM

    zeros = jnp.zeros((hd, L), BF16)
    for hq in range(ATT_HEADS):
        q = qT_ref[hq * hd:(hq + 1) * hd, :]
        qz_s[hq] = jnp.concatenate([q, zeros] if (hq // gsz) % 2 == 0 else [zeros, q], axis=0)

    n_i = lax.broadcasted_iota(jnp.int32, (nb, L), 0)
    for hq in range(ATT_HEADS):
        j = (hq // gsz) // 2
        gate = jnp.dot(km_ref[:, j * LANES:(j + 1) * LANES].astype(BF16), qz_s[hq], preferred_element_type=F32)
        bias_s[hq] = _top_k_bias(gate, n_i, n_i < t, 0)

    def scores(hq, n):
        j = (hq // gsz) // 2
        return jnp.dot(kb_ref[n, :, j * LANES:(j + 1) * LANES], qz_s[hq], preferred_element_type=F32)

    def v_rows(hq, n):
        kvh = hq // gsz
        return vT_ref[n, kvh * V_EXT:(kvh + 1) * V_EXT, :]

    causalT = lax.broadcasted_iota(jnp.int32, (L, L), 0) <= lax.broadcasted_iota(jnp.int32, (L, L), 1)

    def stage_scores(n, half, own):
        keys = slice(half * L, (half + 1) * L)
        for hq in range(ATT_HEADS):
            s = scores(hq, n).astype(BF16)
            if own:
                s_s[hq, keys, :] = jnp.where(causalT, s, MASKED)
            else:
                s_s[hq, keys, :] = s + bias_s[hq, pl.ds(n, 1), :].astype(BF16)

    def softmax_and_pv(n, n_blocks, own):
        keys = slice(0, n_blocks * L)
        for hq in range(ATT_HEADS):
            m_new = jnp.max(s_s[hq, keys, :], axis=0, keepdims=True).astype(F32)
            if not own:
                m_old = m_s[hq]
                m_new = jnp.maximum(m_old, m_new)
                a_s[hq] = jnp.exp2(m_old - m_new)
            p_s[hq, keys, :] = jnp.exp2(s_s[hq, keys, :] - m_new.astype(BF16))
            m_s[hq] = m_new
        for hq in range(ATT_HEADS):
            rows = slice(hq * V_EXT, (hq + 1) * V_EXT)
            v = jnp.concatenate([v_rows(hq, n + b) for b in range(n_blocks)], axis=1)
            pv = jnp.dot(v, p_s[hq, keys, :], preferred_element_type=F32)
            acc_s[rows, :] = pv if own else a_s[hq] * acc_s[rows, :] + pv

    stage_scores(t, 0, True)
    softmax_and_pv(t, 1, True)

    def past_pair(i, carry):
        stage_scores(2 * i, 0, False)
        stage_scores(2 * i + 1, 1, False)
        softmax_and_pv(2 * i, 2, False)
        return carry

    lax.fori_loop(0, t // 2, past_pair, 0)

    @pl.when(t % 2 == 1)
    def _():
        stage_scores(t - 1, 0, False)
        softmax_and_pv(t - 1, 1, False)

    for hq in range(ATT_HEADS):
        rows = slice(hq * hd, (hq + 1) * hd)
        og_s[rows, :] = acc_s[hq * V_EXT:hq * V_EXT + hd, :] / acc_s[hq * V_EXT + hd:hq * V_EXT + hd + 1, :] * gT_ref[rows, :]
    og = og_s[...].T.astype(BF16)
    o_ref[...] = r_ref[...] + jnp.dot(og, wo_ref[...], preferred_element_type=F32)


def _attn_prompt(qT, gT, res, kb, vT, km, wo):
    S = res.shape[0]
    nb = S // CHUNK
    col_blk = pl.BlockSpec((ATT_WIDTH, CHUNK), lambda i: (0, i))
    row_blk = pl.BlockSpec((CHUNK, D_MODEL), lambda i: (i, 0))
    scratch = [pltpu.VMEM((ATT_HEADS, 2 * HEAD_DIM, CHUNK), BF16), pltpu.VMEM((ATT_HEADS, nb, CHUNK), F32),
               pltpu.VMEM((ATT_HEADS, 1, CHUNK), F32), pltpu.VMEM((ATT_HEADS, 1, CHUNK), F32),
               pltpu.VMEM((ATT_HEADS * V_EXT, CHUNK), F32), pltpu.VMEM((ATT_WIDTH, CHUNK), F32),
               pltpu.VMEM((ATT_HEADS, 2 * CHUNK, CHUNK), BF16), pltpu.VMEM((ATT_HEADS, 2 * CHUNK, CHUNK), BF16)]
    return pl.pallas_call(
        _attn_prompt_kernel, grid=(nb,),
        in_specs=[col_blk, col_blk, row_blk, _const_spec(kb.shape), _const_spec(vT.shape), _const_spec(km.shape),
                  _const_spec(wo.shape)],
        out_specs=row_blk, out_shape=jax.ShapeDtypeStruct((S, D_MODEL), F32), scratch_shapes=scratch,
        name="attn_prompt",
        compiler_params=pltpu.CompilerParams(dimension_semantics=("arbitrary",), vmem_limit_bytes=VMEM_LIMIT),
    )(qT, gT, res, kb, vT, km, wo)


def _pair_rms(x, g2):
    lo = lax.broadcasted_iota(jnp.int32, (x.shape[0], LANES), 1) < HEAD_DIM
    out = []
    for c in range(0, x.shape[1], LANES):
        xt = x[:, c:c + LANES]
        sq = xt * xt
        s_lo = jnp.sum(jnp.where(lo, sq, 0.0), axis=-1, keepdims=True)
        s_hi = jnp.sum(jnp.where(lo, 0.0, sq), axis=-1, keepdims=True)
        r = jnp.where(lo, lax.rsqrt(s_lo / HEAD_DIM + NORM_EPS), lax.rsqrt(s_hi / HEAD_DIM + NORM_EPS))
        out.append(xt * r * g2)
    return jnp.concatenate(out, axis=-1)


def _l1_proj_nat_kernel(x_ref, ng_ref, w_ref, qg_ref, kg_ref, q_ref, k_ref, v_ref, g_ref):
    xn = _rms_rows(x_ref[...], ng_ref[...]).astype(BF16)
    u = jnp.dot(xn, w_ref[...], preferred_element_type=F32)
    q_ref[...] = _pair_rms(u[:, R_Q:R_K], qg_ref[...]) * SCALE
    k_ref[...] = _pair_rms(u[:, R_K:R_V], kg_ref[...])
    v_ref[...] = u[:, R_V:R_G]
    g_ref[...] = _silu(u[:, R_G:IN1_ROWS])


def _l1_proj_nat(x, ng, w, qg2, kg2):
    m = x.shape[0]
    shapes = ((m, ATT_WIDTH), (m, KV_WIDTH), (m, KV_WIDTH), (m, ATT_WIDTH))
    return pl.pallas_call(
        _l1_proj_nat_kernel, grid=(1,),
        in_specs=[_const_spec(x.shape), _const_spec(ng.shape), _const_spec(w.shape), _const_spec(qg2.shape),
                  _const_spec(kg2.shape)],
        out_specs=tuple(_const_spec(s) for s in shapes),
        out_shape=tuple(jax.ShapeDtypeStruct(s, F32) for s in shapes), name="l1_proj_nat",
        compiler_params=pltpu.CompilerParams(vmem_limit_bytes=VMEM_LIMIT),
    )(x, ng, w, qg2, kg2)


def _attn_sample_kernel(n_tok, bps, pt_ref, q_ref, kn_ref, vn_ref, *refs):
    ppb = CHUNK // PAGE
    npg = bps * ppb
    k_refs, v_refs, o_ref = refs[:npg], refs[npg:2 * npg], refs[2 * npg]
    qh_s, new_s, s_s, m_s, l_s, g_s, o_s, oh_s = refs[2 * npg + 1:]
    step = pl.program_id(1)
    NB = m_s.shape[0] - 1
    R = 8
    gsz = ATT_HEADS // KV_HEADS
    GR = gsz * R
    hd = HEAD_DIM

    @pl.when(step == 0)
    def _():
        q = q_ref[0]
        for hq in range(ATT_HEADS):
            kvh, g = divmod(hq, gsz)
            qh_s[kvh, g * R:(g + 1) * R, :] = q[:, hq * hd:(hq + 1) * hd]

    lo_lanes = lax.broadcasted_iota(jnp.int32, (GR, LANES), 1) < hd

    def partial_softmax(s):
        m = jnp.max(s, axis=-1, keepdims=True)
        p = jnp.exp(s - m)
        return m, jnp.sum(p, axis=-1, keepdims=True), p.astype(BF16)

    def store_pair(ref, slot, pair, even, odd):
        ref[slot, pair] = jnp.where(lo_lanes, jnp.broadcast_to(even, (GR, LANES)), jnp.broadcast_to(odd, (GR, LANES)))

    def head_T(page_refs, pages, kvh):
        return jnp.concatenate([page_refs[j][0, kvh] for j in pages], axis=-1).astype(BF16)

    def widen_T(vT_h, kvh):
        zeros = jnp.zeros(vT_h.shape, vT_h.dtype)
        return jnp.concatenate([vT_h, zeros] if kvh % 2 == 0 else [zeros, vT_h], axis=0)

    for kvh in range(KV_HEADS):
        s_s[kvh] = jnp.dot(qh_s[kvh].astype(BF16), head_T(k_refs, range(npg), kvh), preferred_element_type=F32)
    for pair in range(KV_HEADS // 2):
        for i in range(bps):
            blk = step * bps + i
            stats, out = [], 0.0
            for kvh in (2 * pair, 2 * pair + 1):
                s = s_s[kvh, :, i * CHUNK:(i + 1) * CHUNK]
                m, l, p = partial_softmax(s)
                stats.append((m, l, jnp.sum(s, axis=-1, keepdims=True)))
                vT = widen_T(head_T(v_refs, range(i * ppb, (i + 1) * ppb), kvh), kvh)
                out = out + lax.dot_general(p, vT, _NT, preferred_element_type=F32)
            store_pair(m_s, blk, pair, stats[0][0], stats[1][0])
            store_pair(l_s, blk, pair, stats[0][1], stats[1][1])
            store_pair(g_s, blk, pair, stats[0][2], stats[1][2])
            o_s[blk, pair] = out

    @pl.when(step == pl.num_programs(1) - 1)
    def _():
        new_s[...] = jnp.zeros(new_s.shape, F32)
        kn = kn_ref[0]
        vn = vn_ref[0]
        row_tok = lax.broadcasted_iota(jnp.int32, (GR, LANES), 0) % R
        key = lax.broadcasted_iota(jnp.int32, (GR, LANES), 1)
        own_ok = jnp.logical_and(key <= row_tok, key < n_tok)
        for pair in range(KV_HEADS // 2):
            stats, out = [], 0.0
            for kvh in (2 * pair, 2 * pair + 1):
                new_s[0, kvh, 0:R, :] = kn[:, kvh * hd:(kvh + 1) * hd]
                new_s[1, kvh, 0:R, :] = vn[:, kvh * hd:(kvh + 1) * hd]
                s = lax.dot_general(qh_s[kvh].astype(BF16), new_s[0, kvh].astype(BF16), _NT,
                                    preferred_element_type=F32)
                m, l, p = partial_softmax(jnp.where(own_ok, s, MASKED))
                stats.append((m, l))
                v = new_s[1, kvh].astype(BF16)
                zeros = jnp.zeros(v.shape, v.dtype)
                v = jnp.concatenate([v, zeros] if kvh % 2 == 0 else [zeros, v], axis=-1)
                out = out + jnp.dot(p, v, preferred_element_type=F32)
            store_pair(m_s, NB, pair, stats[0][0], stats[1][0])
            store_pair(l_s, NB, pair, stats[0][1], stats[1][1])
            o_s[NB, pair] = out

        for pair in range(KV_HEADS // 2):
            v1 = v2 = v3 = jnp.full((GR, LANES), -jnp.inf, F32)
            i1 = i2 = i3 = jnp.full((GR, LANES), -1.0, F32)
            for b in range(NB):
                g = g_s[b, pair]
                c1, c2, c3 = g > v1, g > v2, g > v3
                v3, i3 = jnp.where(c2, v2, jnp.where(c3, g, v3)), jnp.where(c2, i2, jnp.where(c3, float(b), i3))
                v2, i2 = jnp.where(c1, v1, jnp.where(c2, g, v2)), jnp.where(c1, i1, jnp.where(c2, float(b), i2))
                v1, i1 = jnp.where(c1, g, v1), jnp.where(c1, float(b), i1)
            picked = lambda b: jnp.logical_or(jnp.logical_or(i1 == float(b), i2 == float(b)), i3 == float(b))
            m_tot = m_s[NB, pair]
            for b in range(NB):
                m_tot = jnp.maximum(m_tot, jnp.where(picked(b), m_s[b, pair], MASKED))
            w = jnp.exp(m_s[NB, pair] - m_tot)
            num = w * o_s[NB, pair]
            den = w * l_s[NB, pair]
            for b in range(NB):
                w = jnp.where(picked(b), jnp.exp(m_s[b, pair] - m_tot), 0.0)
                num = num + w * o_s[b, pair]
                den = den + w * l_s[b, pair]
            oh_s[pair] = num / den
        lo = lax.broadcasted_iota(jnp.int32, (R, LANES), 1) < hd
        for tile in range(ATT_HEADS // 2):
            kvh, g = divmod(2 * tile, gsz)
            first = oh_s[kvh // 2, g * R:(g + 1) * R, :]
            second = oh_s[kvh // 2, (g + 1) * R:(g + 2) * R, :]
            if kvh % 2 == 0:
                second = pltpu.roll(second, hd, 1)
            else:
                first = pltpu.roll(first, hd, 1)
            o_ref[0, :, tile * LANES:(tile + 1) * LANES] = jnp.where(lo, first, second)


def _attn_sample(page_table, q8, kn8, vn8, cache_kT, cache_vT, n_tok):
    B, n_pages = page_table.shape
    ppb = CHUNK // PAGE
    assert n_pages % ppb == 0, "the new tokens' block is assumed to hold no cached keys"
    nblk = n_pages // ppb
    bps = SAMPLE_BLOCKS_PER_STEP if nblk % SAMPLE_BLOCKS_PER_STEP == 0 else 1
    npg = bps * ppb
    GR = ATT_HEADS // KV_HEADS * 8
    seq_blk = lambda w: pl.BlockSpec((1, 8, w), lambda b, s, pt: (b, 0, 0))
    page_blk = lambda i: pl.BlockSpec((1, KV_HEADS, HEAD_DIM, PAGE), lambda b, s, pt: (pt[b, npg * s + i], 0, 0, 0))
    pages = [page_blk(i) for i in range(npg)]
    part = lambda: pltpu.VMEM((nblk + 1, KV_HEADS // 2, GR, LANES), F32)
    grid_spec = pltpu.PrefetchScalarGridSpec(
        num_scalar_prefetch=1, grid=(B, nblk // bps),
        in_specs=[seq_blk(ATT_WIDTH), seq_blk(KV_WIDTH), seq_blk(KV_WIDTH)] + pages + pages,
        out_specs=seq_blk(ATT_WIDTH),
        scratch_shapes=[pltpu.VMEM((KV_HEADS, GR, HEAD_DIM), F32), pltpu.VMEM((2, KV_HEADS, LANES, HEAD_DIM), F32),
                        pltpu.VMEM((KV_HEADS, GR, bps * CHUNK), F32), part(), part(), part(), part(),
                        pltpu.VMEM((KV_HEADS // 2, GR, LANES), F32)])
    return pl.pallas_call(
        functools.partial(_attn_sample_kernel, n_tok, bps), grid_spec=grid_spec,
        out_shape=jax.ShapeDtypeStruct((B, 8, ATT_WIDTH), F32), name="attn_sample",
        compiler_params=pltpu.CompilerParams(dimension_semantics=("arbitrary", "arbitrary"), vmem_limit_bytes=VMEM_LIMIT),
    )(page_table, q8, kn8, vn8, *([cache_kT] * npg), *([cache_vT] * npg))


def _gated_proj_res_kernel(r_ref, a_ref, g_ref, w_ref, o_ref):
    og = (a_ref[...] * g_ref[...]).astype(BF16)
    o_ref[...] = r_ref[...] + jnp.dot(og, w_ref[...], preferred_element_type=F32)


def _gated_proj_res(res, a, g, w):
    return pl.pallas_call(
        _gated_proj_res_kernel, grid=(1,),
        in_specs=[_const_spec(res.shape), _const_spec(a.shape), _const_spec(g.shape), _const_spec(w.shape)],
        out_specs=_const_spec(res.shape), out_shape=jax.ShapeDtypeStruct(res.shape, F32), name="gated_proj_res",
        compiler_params=pltpu.CompilerParams(vmem_limit_bytes=VMEM_LIMIT),
    )(res, a, g, w)


def _attention_layer(yp, ys, cache_k, cache_v, page_table, norm_g, w_in, qn_g, kn_g, w_out):
    S = yp.shape[1]
    B, T, _ = ys.shape
    ng = norm_g.reshape(1, -1)
    wo = w_out.astype(BF16)
    qT, gT, k_p, v_p, kb, vT, km = _l1_proj(yp.reshape(S, D_MODEL), ng, w_in.T.astype(BF16),
                                            qn_g.reshape(-1, 1), kn_g.reshape(-1, 1))
    yp2 = _attn_prompt(qT, gT, yp.reshape(S, D_MODEL), kb, vT, km.reshape(-1, KV_WIDTH), wo)

    ys2 = ys.reshape(B * T, D_MODEL)
    pair = lambda g: jnp.concatenate([g, g]).reshape(1, LANES)
    q_s, k_s, v_s, g_s = _l1_proj_nat(ys2, ng, w_in.astype(BF16), pair(qn_g), pair(kn_g))
    pad_tok = lambda a: jnp.pad(a.reshape(B, T, -1), ((0, 0), (0, 8 - T), (0, 0)))
    paged = lambda c: jnp.transpose(c, (0, 2, 3, 1))
    o_s = _attn_sample(page_table, pad_tok(q_s), pad_tok(k_s), pad_tok(v_s), paged(cache_k), paged(cache_v), T)
    ys3 = _gated_proj_res(ys2, o_s[:, :T].reshape(B * T, ATT_WIDTH), g_s, wo)
    tok_major = lambda aT: jnp.transpose(aT.reshape(KV_HEADS, HEAD_DIM, S), (2, 0, 1))[None]
    return (yp2.reshape(1, S, D_MODEL), ys3.reshape(B, T, D_MODEL), tok_major(k_p), tok_major(v_p),
            k_s.reshape(B, T, KV_HEADS, HEAD_DIM), v_s.reshape(B, T, KV_HEADS, HEAD_DIM))


def kernel(x_prompt, x_sample, state_ssm, state_ssd_conv, state_conf_conv, cache_k, cache_v, page_table, norm0_g, w_in0, ssd_conv_w, ssd_conv_b, ssd_dt_bias, ssd_a_log, ssd_d, ssd_norm_g, conf_conv_w, conf_conv_b, conf_ln_g, conf_ln_b, w_out0, norm1_g, w_in1, q_norm_g, k_norm_g, w_out1):
    p0 = _prep_l0_params(norm0_g[0], w_in0[0], ssd_conv_w[0], ssd_conv_b[0], ssd_dt_bias[0], ssd_a_log[0], ssd_d[0],
                         ssd_norm_g[0], conf_conv_w[0], conf_conv_b[0], conf_ln_g[0], conf_ln_b[0], w_out0[0])
    yp, ys, h_p, h_s, sh_p, sh_s, ch_p, ch_s = _hybrid_layer(
        x_prompt, x_sample, state_ssm[0], state_ssd_conv[0], state_conf_conv[0], p0)
    yp, ys, k_p, v_p, k_s, v_s = _attention_layer(yp, ys, cache_k[0], cache_v[0], page_table, norm1_g[0], w_in1[0],
                                                  q_norm_g[0], k_norm_g[0], w_out1[0])
    return (yp, ys, h_p[None], h_s[None], sh_p[None], sh_s[None], ch_p[None], ch_s[None],
            k_p[None], v_p[None], k_s[None], v_s[None])
```

```python
import functools

import jax
import jax.numpy as jnp
from jax import lax
from jax.experimental import pallas as pl
from jax.experimental.pallas import tpu as pltpu

F32 = jnp.float32
BF16 = jnp.bfloat16
NORM_EPS = 1e-6
MASKED = -1e30
LANES = 128
SUBLANES = 8
CHUNK = 256
D_MODEL = 1024
SSD_HEADS = 16
SSD_HEAD_DIM = 64
SSD_GROUPS = 4
SSD_STATE = 128
SSD_CONV = 4
SSD_CONV_DIM = 2048
CONF_CONV = 31
ATT_HEADS = 16
KV_HEADS = 4
HEAD_DIM = 64
KV_WIDTH = KV_HEADS * HEAD_DIM
MOBA_TOP_K = 3
PAGE = 128
C_Z, C_XBC, C_GA, C_GB, C_CG, C_DT, IN0_PAD = 0, 1024, 3072, 4096, 5120, 6144, 6272
SSD_HIST_PAD = 8
CONF_HIST_PAD = 32
SAMPLE_CHUNK = 128
SAMPLE_BLOCKS_PER_STEP = 8
VMEM_LIMIT = 56 * 1024 * 1024

_NT = (((1,), (1,)), ((), ()))


def _sigmoid(x):
    return 1.0 / (1.0 + jnp.exp(-x))


def _silu(x):
    return x * _sigmoid(x)


def _softplus(x):
    return jnp.maximum(x, 0.0) + jnp.log1p(jnp.exp(-jnp.abs(x)))


def _rms_rows(x, g):
    ms = jnp.mean(x * x, axis=-1, keepdims=True)
    return x * lax.rsqrt(ms + NORM_EPS) * g


def _const_spec(shape):
    nd = len(shape)
    return pl.BlockSpec(shape, lambda *_: (0,) * nd)


def _ssd_conv_silu(buf_ref, cw_ref, cb_ref, rows, out_ref):
    for c in range(0, SSD_CONV_DIM, 512):
        acc = cb_ref[:, c:c + 512]
        for k in range(SSD_CONV):
            o = SSD_HIST_PAD - (SSD_CONV - 1) + k
            acc = acc + cw_ref[k:k + 1, c:c + 512] * buf_ref[o:o + rows, c:c + 512]
        out_ref[:, c:c + 512] = _silu(acc)


def _gated_group_norm(y, z, sng_ref, yc_ref):
    y = y * _silu(z)
    gw = D_MODEL // SSD_GROUPS
    for g in range(SSD_GROUPS):
        yg = y[:, g * gw:(g + 1) * gw]
        r = lax.rsqrt(jnp.mean(yg * yg, axis=-1, keepdims=True) + NORM_EPS)
        yc_ref[:, g * gw:(g + 1) * gw] = (yg * r * sng_ref[:, g * gw:(g + 1) * gw]).astype(yc_ref.dtype)


CONV_WIN_EXTRA = (CONF_CONV - 1) // SUBLANES * SUBLANES


def _conv_window_scratch(rows):
    return pltpu.VMEM((SUBLANES, min(rows, 128) + CONV_WIN_EXTRA, LANES), F32)


def _conformer(gl_ref, rows, cgate, ccw_ref, ccb_ref, lng_ref, lnb_ref, conv_ref, win_ref, yc_ref):
    first = CONF_HIST_PAD - (CONF_CONV - 1)
    rb = min(rows, 128)
    for c in range(0, D_MODEL, LANES):
        for r0 in range(0, rows, rb):
            acc = jnp.broadcast_to(ccb_ref[:, c:c + LANES], (rb, LANES))
            for phase in range(SUBLANES):
                taps = [k for k in range(CONF_CONV) if (first + k) % SUBLANES == phase]
                if not taps:
                    continue
                lo, hi = first + taps[0], first + taps[-1]
                n_win = hi - lo + rb
                win_ref[phase, 0:n_win, :] = gl_ref[r0 + lo:r0 + hi + rb, c:c + LANES]
                for k in taps:
                    off = first + k - lo
                    acc = acc + ccw_ref[k:k + 1, c:c + LANES] * win_ref[phase, off:off + rb, :]
            conv_ref[r0:r0 + rb, c:c + LANES] = acc
    cv = conv_ref[...]
    mu = jnp.mean(cv, axis=-1, keepdims=True)
    xc = cv - mu
    var = jnp.mean(xc * xc, axis=-1, keepdims=True)
    cn = xc * lax.rsqrt(var + NORM_EPS) * lng_ref[...] + lnb_ref[...]
    yc_ref[:, D_MODEL:2 * D_MODEL] = (_silu(cn) * _silu(cgate)).astype(yc_ref.dtype)


def _ssd_chunk(L, act_ref, dt, alog_ref, dcol_ref, h_in_ref, h_out_ref, yT_ref):
    a = dt * (-jnp.exp(alog_ref[...]))
    r_i = lax.broadcasted_iota(jnp.int32, (L, L), 0)
    c_i = lax.broadcasted_iota(jnp.int32, (L, L), 1)
    tri = (c_i <= r_i).astype(F32)
    acs = jnp.dot(tri, a, precision=lax.Precision.HIGHEST, preferred_element_type=F32)
    acsT = acs.T
    dtT = dt.T
    last = acsT[:, L - 1:L]
    exp_acsT = jnp.exp(acsT)
    dec_endT = jnp.exp(last - acsT)
    chunk_dec = jnp.exp(last)
    causalT = r_i <= c_i

    xT = act_ref[:, 0:D_MODEL].T
    P = SSD_HEAD_DIM
    hpg = SSD_HEADS // SSD_GROUPS
    gp = hpg * P
    for g in range(SSD_GROUPS):
        b_g = act_ref[:, D_MODEL + g * SSD_STATE:D_MODEL + (g + 1) * SSD_STATE].astype(BF16)
        c_g = act_ref[:, D_MODEL + (SSD_GROUPS + g) * SSD_STATE:D_MODEL + (SSD_GROUPS + g + 1) * SSD_STATE].astype(BF16)
        cbT = lax.dot_general(b_g, c_g, _NT, preferred_element_type=F32)
        h_g = h_in_ref[g * gp:(g + 1) * gp, :]
        y_offT = lax.dot_general(h_g.astype(BF16), c_g, _NT, preferred_element_type=F32)
        st_lhs = []
        for r in range(hpg):
            h = g * hpg + r
            rows = slice(h * P, (h + 1) * P)
            xT_h = xT[rows, :]
            xdtT_h = xT_h * dtT[h:h + 1, :]
            seg = acsT[h:h + 1, :] - acs[:, h:h + 1]
            mT = (jnp.exp(jnp.where(causalT, seg, -jnp.inf)) * cbT).astype(BF16)
            y_dT = jnp.dot(xdtT_h.astype(BF16), mT, preferred_element_type=F32)
            y_oT = y_offT[r * P:(r + 1) * P, :] * exp_acsT[h:h + 1, :]
            yT_ref[rows, :] = y_dT + y_oT + dcol_ref[rows, :] * xT_h
            st_lhs.append((xdtT_h * dec_endT[h:h + 1, :]).astype(BF16))
        st = jnp.dot(jnp.concatenate(st_lhs, axis=0), b_g, preferred_element_type=F32)
        for r in range(hpg):
            h = g * hpg + r
            h_out_ref[h * P:(h + 1) * P, :] = h_g[r * P:(r + 1) * P, :] * chunk_dec[h:h + 1, :] + st[r * P:(r + 1) * P, :]


def _l0_prompt_kernel(x_ref, ng_ref, w_ref, cw_ref, cb_ref, dtb_ref, alog_ref, dcol_ref, sng_ref,
                      ccw_ref, ccb_ref, lng_ref, lnb_ref, wo_ref, h0_ref, sh0_ref, ch0_ref,
                      y_ref, hT_ref, sh_ref, chh_ref,
                      ug_s, ucg_s, uz_s, xbc_s, gl_s, h_s, act_s, yT_s, conv_s, win_s, yc_s):
    i = pl.program_id(0)
    L = CHUNK

    @pl.when(i == 0)
    def _():
        h_s[...] = h0_ref[...]
        xbc_s[0:SSD_HIST_PAD, :] = sh0_ref[...]
        gl_s[0:CONF_HIST_PAD, :] = ch0_ref[...]

    x = x_ref[...]
    xn = _rms_rows(x, ng_ref[...]).astype(BF16)

    def project(col, width, dst_ref, row0=0):
        for c in range(0, width, 512):
            dst_ref[row0:row0 + L, c:c + 512] = jnp.dot(xn, w_ref[:, col + c:col + c + 512], preferred_element_type=F32)

    project(C_XBC, SSD_CONV_DIM, xbc_s, SSD_HIST_PAD)
    dt_raw = jnp.dot(xn, w_ref[:, C_DT:C_DT + LANES], preferred_element_type=F32)
    project(C_GA, 2 * D_MODEL, ug_s)
    project(C_CG, D_MODEL, ucg_s)
    project(C_Z, D_MODEL, uz_s)

    _ssd_conv_silu(xbc_s, cw_ref, cb_ref, L, act_s)
    dt = _softplus(dt_raw + dtb_ref[...])
    _ssd_chunk(L, act_s, dt, alog_ref, dcol_ref, h_s, h_s, yT_s)
    _gated_group_norm(yT_s[...].T, uz_s[...], sng_ref, yc_s)

    gl_s[CONF_HIST_PAD:CONF_HIST_PAD + L, :] = ug_s[:, 0:D_MODEL] * _sigmoid(ug_s[:, D_MODEL:2 * D_MODEL])
    _conformer(gl_s, L, ucg_s[...], ccw_ref, ccb_ref, lng_ref, lnb_ref, conv_s, win_s, yc_s)

    y_ref[...] = x + jnp.dot(yc_s[...], wo_ref[...], preferred_element_type=F32)

    xbc_s[0:SSD_HIST_PAD, :] = xbc_s[L:L + SSD_HIST_PAD, :]
    gl_s[0:CONF_HIST_PAD, :] = gl_s[L:L + CONF_HIST_PAD, :]

    @pl.when(i == pl.num_programs(0) - 1)
    def _():
        hT_ref[...] = h_s[...]
        sh_ref[...] = xbc_s[0:SSD_HIST_PAD, :]
        chh_ref[...] = gl_s[0:CONF_HIST_PAD, :]


def _l0_prompt(x, p, h0, sh0, ch0):
    S = x.shape[0]
    assert S % CHUNK == 0
    nc = S // CHUNK
    row_blk = lambda w: pl.BlockSpec((CHUNK, w), lambda i: (i, 0))
    in_specs = [row_blk(D_MODEL), _const_spec((1, D_MODEL)), _const_spec((D_MODEL, IN0_PAD)),
                _const_spec((SSD_CONV, SSD_CONV_DIM)), _const_spec((1, SSD_CONV_DIM)),
                _const_spec((1, LANES)), _const_spec((1, LANES)), _const_spec((D_MODEL, 1)), _const_spec((1, D_MODEL)),
                _const_spec((CONF_CONV, D_MODEL)), _const_spec((1, D_MODEL)), _const_spec((1, D_MODEL)),
                _const_spec((1, D_MODEL)), _const_spec((2 * D_MODEL, D_MODEL)),
                _const_spec((D_MODEL, SSD_STATE)), _const_spec((SSD_HIST_PAD, SSD_CONV_DIM)),
                _const_spec((CONF_HIST_PAD, D_MODEL))]
    out_shape = (jax.ShapeDtypeStruct((S, D_MODEL), F32), jax.ShapeDtypeStruct((D_MODEL, SSD_STATE), F32),
                 jax.ShapeDtypeStruct((SSD_HIST_PAD, SSD_CONV_DIM), F32), jax.ShapeDtypeStruct((CONF_HIST_PAD, D_MODEL), F32))
    out_specs = (row_blk(D_MODEL), _const_spec((D_MODEL, SSD_STATE)), _const_spec((SSD_HIST_PAD, SSD_CONV_DIM)),
                 _const_spec((CONF_HIST_PAD, D_MODEL)))
    scratch = [pltpu.VMEM((CHUNK, 2 * D_MODEL), F32), pltpu.VMEM((CHUNK, D_MODEL), F32), pltpu.VMEM((CHUNK, D_MODEL), F32),
               pltpu.VMEM((SSD_HIST_PAD + CHUNK, SSD_CONV_DIM), F32),
               pltpu.VMEM((CONF_HIST_PAD + CHUNK, D_MODEL), F32), pltpu.VMEM((D_MODEL, SSD_STATE), F32),
               pltpu.VMEM((CHUNK, SSD_CONV_DIM), F32), pltpu.VMEM((D_MODEL, CHUNK), F32),
               pltpu.VMEM((CHUNK, D_MODEL), F32), _conv_window_scratch(CHUNK), pltpu.VMEM((CHUNK, 2 * D_MODEL), BF16)]
    return pl.pallas_call(
        _l0_prompt_kernel, grid=(nc,), in_specs=in_specs, out_specs=out_specs, out_shape=out_shape,
        scratch_shapes=scratch, name="l0_prompt",
        compiler_params=pltpu.CompilerParams(dimension_semantics=("arbitrary",), vmem_limit_bytes=VMEM_LIMIT),
    )(x, p["ng"], p["w_in"], p["cw"], p["cb"], p["dtb"], p["alog"], p["dcol"], p["sng"],
      p["ccw"], p["ccb"], p["lng"], p["lnb"], p["w_out"], h0, sh0, ch0)


def _norm_proj_kernel(x_ref, g_ref, w_ref, o_ref):
    xn = _rms_rows(x_ref[...], g_ref[...]).astype(BF16)
    o_ref[...] = jnp.dot(xn, w_ref[...], preferred_element_type=F32)


def _norm_proj(x, g, w):
    m, n = x.shape[0], w.shape[1]
    return pl.pallas_call(
        _norm_proj_kernel, grid=(1,),
        in_specs=[_const_spec(x.shape), _const_spec(g.shape), _const_spec(w.shape)],
        out_specs=_const_spec((m, n)), out_shape=jax.ShapeDtypeStruct((m, n), F32), name="norm_proj",
        compiler_params=pltpu.CompilerParams(vmem_limit_bytes=VMEM_LIMIT),
    )(x, g, w)


def _proj_res_kernel(r_ref, a_ref, w_ref, o_ref):
    o_ref[...] = r_ref[...] + jnp.dot(a_ref[...].astype(BF16), w_ref[...], preferred_element_type=F32)


def _proj_res(res, a, w):
    return pl.pallas_call(
        _proj_res_kernel, grid=(1,),
        in_specs=[_const_spec(res.shape), _const_spec(a.shape), _const_spec(w.shape)],
        out_specs=_const_spec(res.shape), out_shape=jax.ShapeDtypeStruct(res.shape, F32), name="proj_res",
        compiler_params=pltpu.CompilerParams(vmem_limit_bytes=VMEM_LIMIT),
    )(res, a, w)


def _l0_sample_kernel(n_tok, u_ref, cw_ref, cb_ref, dtb_ref, alog_ref, dcol_ref, sng_ref,
                      ccw_ref, ccb_ref, lng_ref, lnb_ref, h0_ref, sh0_ref, ch0_ref,
                      yc_ref, hT_ref, sh_ref, chh_ref,
                      xbc_s, gl_s, act_s, yT_s, conv_s, win_s):
    R = 8
    L = SAMPLE_CHUNK
    u = u_ref[0]
    xbc_s[0:SSD_HIST_PAD, :] = sh0_ref[0]
    xbc_s[SSD_HIST_PAD:SSD_HIST_PAD + R, :] = u[:, C_XBC:C_XBC + SSD_CONV_DIM]
    act_s[R:L, :] = jnp.zeros((L - R, SSD_CONV_DIM), F32)
    _ssd_conv_silu(xbc_s, cw_ref, cb_ref, R, act_s.at[0:R])

    dt = _softplus(u[:, C_DT:C_DT + LANES] + dtb_ref[...])
    dt = jnp.where(lax.broadcasted_iota(jnp.int32, (R, LANES), 0) < n_tok, dt, 0.0)
    dt = jnp.concatenate([dt, jnp.zeros((L - R, LANES), F32)], axis=0)
    _ssd_chunk(L, act_s, dt, alog_ref, dcol_ref, h0_ref.at[0], hT_ref.at[0], yT_s)
    y = yT_s[...].T[0:R, :]
    _gated_group_norm(y, u[:, C_Z:C_Z + D_MODEL], sng_ref, yc_ref.at[0])

    gl_s[0:CONF_HIST_PAD, :] = ch0_ref[0]
    gl_s[CONF_HIST_PAD:CONF_HIST_PAD + R, :] = u[:, C_GA:C_GA + D_MODEL] * _sigmoid(u[:, C_GB:C_GB + D_MODEL])
    _conformer(gl_s, R, u[:, C_CG:C_CG + D_MODEL], ccw_ref, ccb_ref, lng_ref, lnb_ref, conv_s, win_s, yc_ref.at[0])

    sh_ref[0] = xbc_s[...]
    chh_ref[0] = gl_s[...]


def _l0_sample(u, p, h0, sh0, ch0, n_tok):
    B = u.shape[0]
    seq_blk = lambda *s: pl.BlockSpec((1,) + s, lambda b: (b,) + (0,) * len(s))
    in_specs = [seq_blk(8, IN0_PAD),
                _const_spec((SSD_CONV, SSD_CONV_DIM)), _const_spec((1, SSD_CONV_DIM)),
                _const_spec((1, LANES)), _const_spec((1, LANES)), _const_spec((D_MODEL, 1)), _const_spec((1, D_MODEL)),
                _const_spec((CONF_CONV, D_MODEL)), _const_spec((1, D_MODEL)), _const_spec((1, D_MODEL)),
                _const_spec((1, D_MODEL)),
                seq_blk(D_MODEL, SSD_STATE), seq_blk(SSD_HIST_PAD, SSD_CONV_DIM), seq_blk(CONF_HIST_PAD, D_MODEL)]
    out_shape = (jax.ShapeDtypeStruct((B, 8, 2 * D_MODEL), F32), jax.ShapeDtypeStruct((B, D_MODEL, SSD_STATE), F32),
                 jax.ShapeDtypeStruct((B, SSD_HIST_PAD + 8, SSD_CONV_DIM), F32),
                 jax.ShapeDtypeStruct((B, CONF_HIST_PAD + 8, D_MODEL), F32))
    out_specs = (seq_blk(8, 2 * D_MODEL), seq_blk(D_MODEL, SSD_STATE), seq_blk(SSD_HIST_PAD + 8, SSD_CONV_DIM),
                 seq_blk(CONF_HIST_PAD + 8, D_MODEL))
    scratch = [pltpu.VMEM((SSD_HIST_PAD + 8, SSD_CONV_DIM), F32), pltpu.VMEM((CONF_HIST_PAD + 8, D_MODEL), F32),
               pltpu.VMEM((SAMPLE_CHUNK, SSD_CONV_DIM), F32),
               pltpu.VMEM((D_MODEL, SAMPLE_CHUNK), F32), pltpu.VMEM((8, D_MODEL), F32), _conv_window_scratch(8)]
    return pl.pallas_call(
        functools.partial(_l0_sample_kernel, n_tok), grid=(B,), in_specs=in_specs, out_specs=out_specs,
        out_shape=out_shape, scratch_shapes=scratch, name="l0_sample",
        compiler_params=pltpu.CompilerParams(dimension_semantics=("arbitrary",), vmem_limit_bytes=VMEM_LIMIT),
    )(u, p["cw"], p["cb"], p["dtb"], p["alog"], p["dcol"], p["sng"], p["ccw"], p["ccb"], p["lng"], p["lnb"],
      h0, sh0, ch0)


def _prep_l0_params(norm_g, w_in, conv_w, conv_b, dt_bias, a_log, d_skip, ssd_norm_g,
                    cconv_w, cconv_b, cln_g, cln_b, w_out):
    o_dt = D_MODEL + SSD_CONV_DIM
    w = jnp.concatenate([w_in[:, :o_dt], w_in[:, o_dt + SSD_HEADS:], w_in[:, o_dt:o_dt + SSD_HEADS],
                         jnp.zeros((D_MODEL, LANES - SSD_HEADS), w_in.dtype)], axis=1)
    lane_pad = lambda v: jnp.pad(v.astype(F32), (0, LANES - SSD_HEADS)).reshape(1, LANES)
    return dict(ng=norm_g.reshape(1, -1), w_in=w.astype(BF16), cw=conv_w, cb=conv_b.reshape(1, -1),
                dtb=lane_pad(dt_bias), alog=lane_pad(a_log),
                dcol=jnp.repeat(d_skip.astype(F32), SSD_HEAD_DIM).reshape(-1, 1), sng=ssd_norm_g.reshape(1, -1),
                ccw=cconv_w, ccb=cconv_b.reshape(1, -1), lng=cln_g.reshape(1, -1), lnb=cln_b.reshape(1, -1),
                w_out=w_out.astype(BF16))


def _hybrid_layer(x_p, x_s, state_ssm, ssd_hist, conf_hist, p):
    S = x_p.shape[1]
    B, T, _ = x_s.shape
    H, P, N = SSD_HEADS, SSD_HEAD_DIM, SSD_STATE
    zeros = lambda *s: jnp.zeros(s, F32)
    yp, h_p, sh_p, ch_p = _l0_prompt(x_p.reshape(S, D_MODEL), p, zeros(D_MODEL, N), zeros(SSD_HIST_PAD, SSD_CONV_DIM),
                                     zeros(CONF_HIST_PAD, D_MODEL))
    xs2 = x_s.reshape(B * T, D_MODEL)
    u_s = _norm_proj(xs2, p["ng"], p["w_in"]).reshape(B, T, IN0_PAD)
    u_s = jnp.pad(u_s, ((0, 0), (0, 8 - T), (0, 0)))
    sh0 = jnp.pad(ssd_hist, ((0, 0), (SSD_HIST_PAD - (SSD_CONV - 1), 0), (0, 0)))
    ch0 = jnp.pad(conf_hist, ((0, 0), (CONF_HIST_PAD - (CONF_CONV - 1), 0), (0, 0)))
    yc_s, h_s, sh_s, ch_s = _l0_sample(u_s, p, state_ssm.reshape(B, H * P, N), sh0, ch0, T)
    ys = _proj_res(xs2, yc_s[:, :T].reshape(B * T, 2 * D_MODEL), p["w_out"])
    return (yp.reshape(1, S, D_MODEL), ys.reshape(B, T, D_MODEL),
            h_p.reshape(1, H, P, N), h_s.reshape(B, H, P, N),
            sh_p[None, SSD_HIST_PAD - (SSD_CONV - 1):], sh_s[:, SSD_HIST_PAD + T - (SSD_CONV - 1):SSD_HIST_PAD + T],
            ch_p[None, CONF_HIST_PAD - (CONF_CONV - 1):], ch_s[:, CONF_HIST_PAD + T - (CONF_CONV - 1):CONF_HIST_PAD + T])


ATT_WIDTH = ATT_HEADS * HEAD_DIM
R_Q, R_K, R_V, R_G, IN1_ROWS = 0, ATT_WIDTH, ATT_WIDTH + KV_WIDTH, ATT_WIDTH + 2 * KV_WIDTH, 2 * ATT_WIDTH + 2 * KV_WIDTH
SCALE = HEAD_DIM ** -0.5
LOG2_E = 1.4426950408889634
V_EXT = HEAD_DIM + 16


def _head_rms_cols(xT, g_col):
    out = []
    for h in range(xT.shape[0] // HEAD_DIM):
        xh = xT[h * HEAD_DIM:(h + 1) * HEAD_DIM, :]
        r = lax.rsqrt(jnp.mean(xh * xh, axis=0, keepdims=True) + NORM_EPS)
        out.append(xh * r * g_col)
    return out


def _l1_proj_kernel(x_ref, ng_ref, wT_ref, qg_ref, kg_ref, qT_ref, gT_ref, kn_ref, vn_ref, kb_ref, vT_ref, km_ref):
    xn = _rms_rows(x_ref[...], ng_ref[...]).astype(BF16)
    uT = lax.dot_general(wT_ref[...], xn, _NT, preferred_element_type=F32)
    for h, qh in enumerate(_head_rms_cols(uT[R_Q:R_K, :], qg_ref[...])):
        qT_ref[h * HEAD_DIM:(h + 1) * HEAD_DIM, :] = (qh * (SCALE * LOG2_E)).astype(BF16)
    kT = jnp.concatenate(_head_rms_cols(uT[R_K:R_V, :], kg_ref[...]), axis=0)
    kn_ref[...] = kT
    k_nat = kT.T
    kb_ref[0] = k_nat.astype(BF16)
    km_ref[0] = jnp.mean(k_nat, axis=0, keepdims=True)
    vT = uT[R_V:R_G, :]
    vn_ref[...] = vT
    for h in range(KV_HEADS):
        vT_ref[0, h * V_EXT:h * V_EXT + HEAD_DIM, :] = vT[h * HEAD_DIM:(h + 1) * HEAD_DIM, :].astype(BF16)
        vT_ref[0, h * V_EXT + HEAD_DIM:(h + 1) * V_EXT, :] = jnp.ones((V_EXT - HEAD_DIM, vT.shape[1]), BF16)
    gT_ref[...] = _silu(uT[R_G:IN1_ROWS, :])


def _l1_proj(x, ng, wT, qg_col, kg_col):
    S = x.shape[0]
    nb = S // CHUNK
    col_blk = lambda r: pl.BlockSpec((r, CHUNK), lambda i: (0, i))
    row_blk = lambda w: pl.BlockSpec((CHUNK, w), lambda i: (i, 0))
    blk3 = lambda a, b: pl.BlockSpec((1, a, b), lambda i: (i, 0, 0))
    out_shape = (jax.ShapeDtypeStruct((ATT_WIDTH, S), BF16), jax.ShapeDtypeStruct((ATT_WIDTH, S), F32),
                 jax.ShapeDtypeStruct((KV_WIDTH, S), F32), jax.ShapeDtypeStruct((KV_WIDTH, S), F32),
                 jax.ShapeDtypeStruct((nb, CHUNK, KV_WIDTH), BF16), jax.ShapeDtypeStruct((nb, KV_HEADS * V_EXT, CHUNK), BF16),
                 jax.ShapeDtypeStruct((nb, 1, KV_WIDTH), F32))
    out_specs = (col_blk(ATT_WIDTH), col_blk(ATT_WIDTH), col_blk(KV_WIDTH), col_blk(KV_WIDTH),
                 blk3(CHUNK, KV_WIDTH), blk3(KV_HEADS * V_EXT, CHUNK), blk3(1, KV_WIDTH))
    return pl.pallas_call(
        _l1_proj_kernel, grid=(nb,),
        in_specs=[row_blk(D_MODEL), _const_spec((1, D_MODEL)), _const_spec((IN1_ROWS, D_MODEL)),
                  _const_spec((HEAD_DIM, 1)), _const_spec((HEAD_DIM, 1))],
        out_specs=out_specs, out_shape=out_shape, name="l1_proj",
        compiler_params=pltpu.CompilerParams(dimension_semantics=("arbitrary",), vmem_limit_bytes=VMEM_LIMIT),
    )(x, ng, wT, qg_col, kg_col)


def _top_k_bias(gate, idx, n_valid_mask, axis):
    n = gate.shape[axis]
    g = jnp.where(n_valid_mask, gate, -jnp.inf)
    sel = jnp.zeros(gate.shape, F32)
    for _ in range(MOBA_TOP_K):
        mx = jnp.max(g, axis=axis, keepdims=True)
        first = jnp.min(jnp.where(g == mx, idx, n), axis=axis, keepdims=True)
        pick = jnp.logical_and(idx == first, mx > -jnp.inf)
        sel = jnp.where(pick, 1.0, sel)
        g = jnp.where(pick, -jnp.inf, g)
    return jnp.where(sel > 0.0, 0.0, MASKED)


def _attn_prompt_kernel(qT_ref, gT_ref, r_ref, kb_ref, vT_ref, km_ref, wo_ref, o_ref, qz_s, bias_s, m_s, a_s,
                        acc_s, og_s, s_s, p_s):
    t = pl.program_id(0)
    L = CHUNK
    nb = km_ref.shape[0]
    gsz = ATT_HEADS // KV_HEADS
    hd = HEAD_DIM

    zeros = jnp.zeros((hd, L), BF16)
    for hq in range(ATT_HEADS):
        q = qT_ref[hq * hd:(hq + 1) * hd, :]
        qz_s[hq] = jnp.concatenate([q, zeros] if (hq // gsz) % 2 == 0 else [zeros, q], axis=0)

    n_i = lax.broadcasted_iota(jnp.int32, (nb, L), 0)
    for hq in range(ATT_HEADS):
        j = (hq // gsz) // 2
        gate = jnp.dot(km_ref[:, j * LANES:(j + 1) * LANES].astype(BF16), qz_s[hq], preferred_element_type=F32)
        bias_s[hq] = _top_k_bias(gate, n_i, n_i < t, 0)

    def scores(hq, n):
        j = (hq // gsz) // 2
        return jnp.dot(kb_ref[n, :, j * LANES:(j + 1) * LANES], qz_s[hq], preferred_element_type=F32)

    def v_rows(hq, n):
        kvh = hq // gsz
        return vT_ref[n, kvh * V_EXT:(kvh + 1) * V_EXT, :]

    causalT = lax.broadcasted_iota(jnp.int32, (L, L), 0) <= lax.broadcasted_iota(jnp.int32, (L, L), 1)

    def stage_scores(n, half, own):
        for hq in range(ATT_HEADS):
            j = (hq // gsz) // 2
            for r0 in range(0, L, LANES):
                keys = slice(half * L + r0, half * L + r0 + LANES)
                s = jnp.dot(kb_ref[n, r0:r0 + LANES, j * LANES:(j + 1) * LANES], qz_s[hq],
                            preferred_element_type=F32).astype(BF16)
                if own:
                    s_s[hq, keys, :] = jnp.where(causalT[r0:r0 + LANES, :], s, MASKED)
                else:
                    s_s[hq, keys, :] = s + bias_s[hq, pl.ds(n, 1), :].astype(BF16)

    def softmax_and_pv(n, n_blocks, own):
        keys = slice(0, n_blocks * L)
        for hq in range(ATT_HEADS):
            m_new = jnp.max(s_s[hq, keys, :], axis=0, keepdims=True).astype(F32)
            if not own:
                m_old = m_s[hq]
                m_new = jnp.maximum(m_old, m_new)
                a_s[hq] = jnp.exp2(m_old - m_new)
            p_s[hq, keys, :] = jnp.exp2(s_s[hq, keys, :] - m_new.astype(BF16))
            m_s[hq] = m_new
        for hq in range(ATT_HEADS):
            rows = slice(hq * V_EXT, (hq + 1) * V_EXT)
            v = jnp.concatenate([v_rows(hq, n + b) for b in range(n_blocks)], axis=1)
            pv = jnp.dot(v, p_s[hq, keys, :], preferred_element_type=F32)
            acc_s[rows, :] = pv if own else a_s[hq] * acc_s[rows, :] + pv

    stage_scores(t, 0, True)
    softmax_and_pv(t, 1, True)

    def past_pair(i, carry):
        stage_scores(2 * i, 0, False)
        stage_scores(2 * i + 1, 1, False)
        softmax_and_pv(2 * i, 2, False)
        return carry

    lax.fori_loop(0, t // 2, past_pair, 0)

    @pl.when(t % 2 == 1)
    def _():
        stage_scores(t - 1, 0, False)
        softmax_and_pv(t - 1, 1, False)

    for hq in range(ATT_HEADS):
        rows = slice(hq * hd, (hq + 1) * hd)
        og_s[rows, :] = acc_s[hq * V_EXT:hq * V_EXT + hd, :] / acc_s[hq * V_EXT + hd:hq * V_EXT + hd + 1, :] * gT_ref[rows, :]
    og = og_s[...].T.astype(BF16)
    o_ref[...] = r_ref[...] + jnp.dot(og, wo_ref[...], preferred_element_type=F32)


def _attn_prompt(qT, gT, res, kb, vT, km, wo):
    S = res.shape[0]
    nb = S // CHUNK
    col_blk = pl.BlockSpec((ATT_WIDTH, CHUNK), lambda i: (0, i))
    row_blk = pl.BlockSpec((CHUNK, D_MODEL), lambda i: (i, 0))
    scratch = [pltpu.VMEM((ATT_HEADS, 2 * HEAD_DIM, CHUNK), BF16), pltpu.VMEM((ATT_HEADS, nb, CHUNK), F32),
               pltpu.VMEM((ATT_HEADS, 1, CHUNK), F32), pltpu.VMEM((ATT_HEADS, 1, CHUNK), F32),
               pltpu.VMEM((ATT_HEADS * V_EXT, CHUNK), F32), pltpu.VMEM((ATT_WIDTH, CHUNK), F32),
               pltpu.VMEM((ATT_HEADS, 2 * CHUNK, CHUNK), BF16), pltpu.VMEM((ATT_HEADS, 2 * CHUNK, CHUNK), BF16)]
    return pl.pallas_call(
        _attn_prompt_kernel, grid=(nb,),
        in_specs=[col_blk, col_blk, row_blk, _const_spec(kb.shape), _const_spec(vT.shape), _const_spec(km.shape),
                  _const_spec(wo.shape)],
        out_specs=row_blk, out_shape=jax.ShapeDtypeStruct((S, D_MODEL), F32), scratch_shapes=scratch,
        name="attn_prompt",
        compiler_params=pltpu.CompilerParams(dimension_semantics=("arbitrary",), vmem_limit_bytes=VMEM_LIMIT),
    )(qT, gT, res, kb, vT, km, wo)


def _pair_rms(x, g2):
    lo = lax.broadcasted_iota(jnp.int32, (x.shape[0], LANES), 1) < HEAD_DIM
    out = []
    for c in range(0, x.shape[1], LANES):
        xt = x[:, c:c + LANES]
        sq = xt * xt
        s_lo = jnp.sum(jnp.where(lo, sq, 0.0), axis=-1, keepdims=True)
        s_hi = jnp.sum(jnp.where(lo, 0.0, sq), axis=-1, keepdims=True)
        r = jnp.where(lo, lax.rsqrt(s_lo / HEAD_DIM + NORM_EPS), lax.rsqrt(s_hi / HEAD_DIM + NORM_EPS))
        out.append(xt * r * g2)
    return jnp.concatenate(out, axis=-1)


def _l1_proj_nat_kernel(x_ref, ng_ref, w_ref, qg_ref, kg_ref, q_ref, k_ref, v_ref, g_ref):
    xn = _rms_rows(x_ref[...], ng_ref[...]).astype(BF16)
    u = jnp.dot(xn, w_ref[...], preferred_element_type=F32)
    q_ref[...] = _pair_rms(u[:, R_Q:R_K], qg_ref[...]) * SCALE
    k_ref[...] = _pair_rms(u[:, R_K:R_V], kg_ref[...])
    v_ref[...] = u[:, R_V:R_G]
    g_ref[...] = _silu(u[:, R_G:IN1_ROWS])


def _l1_proj_nat(x, ng, w, qg2, kg2):
    m = x.shape[0]
    shapes = ((m, ATT_WIDTH), (m, KV_WIDTH), (m, KV_WIDTH), (m, ATT_WIDTH))
    return pl.pallas_call(
        _l1_proj_nat_kernel, grid=(1,),
        in_specs=[_const_spec(x.shape), _const_spec(ng.shape), _const_spec(w.shape), _const_spec(qg2.shape),
                  _const_spec(kg2.shape)],
        out_specs=tuple(_const_spec(s) for s in shapes),
        out_shape=tuple(jax.ShapeDtypeStruct(s, F32) for s in shapes), name="l1_proj_nat",
        compiler_params=pltpu.CompilerParams(vmem_limit_bytes=VMEM_LIMIT),
    )(x, ng, w, qg2, kg2)


def _attn_sample_kernel(n_tok, bps, pt_ref, q_ref, kn_ref, vn_ref, *refs):
    ppb = CHUNK // PAGE
    npg = bps * ppb
    k_refs, v_refs, o_ref = refs[:npg], refs[npg:2 * npg], refs[2 * npg]
    qh_s, new_s, s_s, m_s, l_s, g_s, o_s, oh_s = refs[2 * npg + 1:]
    step = pl.program_id(1)
    NB = m_s.shape[0] - 1
    R = 8
    gsz = ATT_HEADS // KV_HEADS
    GR = gsz * R
    hd = HEAD_DIM

    @pl.when(step == 0)
    def _():
        q = q_ref[0]
        for hq in range(ATT_HEADS):
            kvh, g = divmod(hq, gsz)
            qh_s[kvh, g * R:(g + 1) * R, :] = q[:, hq * hd:(hq + 1) * hd]

    lo_lanes = lax.broadcasted_iota(jnp.int32, (GR, LANES), 1) < hd

    def partial_softmax(s):
        m = jnp.max(s, axis=-1, keepdims=True)
        p = jnp.exp(s - m)
        return m, jnp.sum(p, axis=-1, keepdims=True), p.astype(BF16)

    def store_pair(ref, slot, pair, even, odd):
        ref[slot, pair] = jnp.where(lo_lanes, jnp.broadcast_to(even, (GR, LANES)), jnp.broadcast_to(odd, (GR, LANES)))

    def head_T(page_refs, pages, kvh):
        return jnp.concatenate([page_refs[j][0, kvh] for j in pages], axis=-1).astype(BF16)

    def widen_T(vT_h, kvh):
        zeros = jnp.zeros(vT_h.shape, vT_h.dtype)
        return jnp.concatenate([vT_h, zeros] if kvh % 2 == 0 else [zeros, vT_h], axis=0)

    for kvh in range(KV_HEADS):
        s_s[kvh] = jnp.dot(qh_s[kvh].astype(BF16), head_T(k_refs, range(npg), kvh), preferred_element_type=F32)
    for pair in range(KV_HEADS // 2):
        for i in range(bps):
            blk = step * bps + i
            stats, out = [], 0.0
            for kvh in (2 * pair, 2 * pair + 1):
                s = s_s[kvh, :, i * CHUNK:(i + 1) * CHUNK]
                m, l, p = partial_softmax(s)
                stats.append((m, l, jnp.sum(s, axis=-1, keepdims=True)))
                vT = widen_T(head_T(v_refs, range(i * ppb, (i + 1) * ppb), kvh), kvh)
                out = out + lax.dot_general(p, vT, _NT, preferred_element_type=F32)
            store_pair(m_s, blk, pair, stats[0][0], stats[1][0])
            store_pair(l_s, blk, pair, stats[0][1], stats[1][1])
            store_pair(g_s, blk, pair, stats[0][2], stats[1][2])
            o_s[blk, pair] = out

    @pl.when(step == pl.num_programs(1) - 1)
    def _():
        new_s[...] = jnp.zeros(new_s.shape, F32)
        kn = kn_ref[0]
        vn = vn_ref[0]
        row_tok = lax.broadcasted_iota(jnp.int32, (GR, LANES), 0) % R
        key = lax.broadcasted_iota(jnp.int32, (GR, LANES), 1)
        own_ok = jnp.logical_and(key <= row_tok, key < n_tok)
        for pair in range(KV_HEADS // 2):
            stats, out = [], 0.0
            for kvh in (2 * pair, 2 * pair + 1):
                new_s[0, kvh, 0:R, :] = kn[:, kvh * hd:(kvh + 1) * hd]
                new_s[1, kvh, 0:R, :] = vn[:, kvh * hd:(kvh + 1) * hd]
                s = lax.dot_general(qh_s[kvh].astype(BF16), new_s[0, kvh].astype(BF16), _NT,
                                    preferred_element_type=F32)
                m, l, p = partial_softmax(jnp.where(own_ok, s, MASKED))
                stats.append((m, l))
                v = new_s[1, kvh].astype(BF16)
                zeros = jnp.zeros(v.shape, v.dtype)
                v = jnp.concatenate([v, zeros] if kvh % 2 == 0 else [zeros, v], axis=-1)
                out = out + jnp.dot(p, v, preferred_element_type=F32)
            store_pair(m_s, NB, pair, stats[0][0], stats[1][0])
            store_pair(l_s, NB, pair, stats[0][1], stats[1][1])
            o_s[NB, pair] = out

        for pair in range(KV_HEADS // 2):
            v1 = v2 = v3 = jnp.full((GR, LANES), -jnp.inf, F32)
            i1 = i2 = i3 = jnp.full((GR, LANES), -1.0, F32)
            for b in range(NB):
                g = g_s[b, pair]
                c1, c2, c3 = g > v1, g > v2, g > v3
                v3, i3 = jnp.where(c2, v2, jnp.where(c3, g, v3)), jnp.where(c2, i2, jnp.where(c3, float(b), i3))
                v2, i2 = jnp.where(c1, v1, jnp.where(c2, g, v2)), jnp.where(c1, i1, jnp.where(c2, float(b), i2))
                v1, i1 = jnp.where(c1, g, v1), jnp.where(c1, float(b), i1)
            picked = lambda b: jnp.logical_or(jnp.logical_or(i1 == float(b), i2 == float(b)), i3 == float(b))
            m_tot = m_s[NB, pair]
            for b in range(NB):
                m_tot = jnp.maximum(m_tot, jnp.where(picked(b), m_s[b, pair], MASKED))
            w = jnp.exp(m_s[NB, pair] - m_tot)
            num = w * o_s[NB, pair]
            den = w * l_s[NB, pair]
            for b in range(NB):
                w = jnp.where(picked(b), jnp.exp(m_s[b, pair] - m_tot), 0.0)
                num = num + w * o_s[b, pair]
                den = den + w * l_s[b, pair]
            oh_s[pair] = num / den
        lo = lax.broadcasted_iota(jnp.int32, (R, LANES), 1) < hd
        for tile in range(ATT_HEADS // 2):
            kvh, g = divmod(2 * tile, gsz)
            first = oh_s[kvh // 2, g * R:(g + 1) * R, :]
            second = oh_s[kvh // 2, (g + 1) * R:(g + 2) * R, :]
            if kvh % 2 == 0:
                second = pltpu.roll(second, hd, 1)
            else:
                first = pltpu.roll(first, hd, 1)
            o_ref[0, :, tile * LANES:(tile + 1) * LANES] = jnp.where(lo, first, second)


def _attn_sample(page_table, q8, kn8, vn8, cache_kT, cache_vT, n_tok):
    B, n_pages = page_table.shape
    ppb = CHUNK // PAGE
    assert n_pages % ppb == 0, "the new tokens' block is assumed to hold no cached keys"
    nblk = n_pages // ppb
    bps = SAMPLE_BLOCKS_PER_STEP if nblk % SAMPLE_BLOCKS_PER_STEP == 0 else 1
    npg = bps * ppb
    GR = ATT_HEADS // KV_HEADS * 8
    seq_blk = lambda w: pl.BlockSpec((1, 8, w), lambda b, s, pt: (b, 0, 0))
    page_blk = lambda i: pl.BlockSpec((1, KV_HEADS, HEAD_DIM, PAGE), lambda b, s, pt: (pt[b, npg * s + i], 0, 0, 0))
    pages = [page_blk(i) for i in range(npg)]
    part = lambda: pltpu.VMEM((nblk + 1, KV_HEADS // 2, GR, LANES), F32)
    grid_spec = pltpu.PrefetchScalarGridSpec(
        num_scalar_prefetch=1, grid=(B, nblk // bps),
        in_specs=[seq_blk(ATT_WIDTH), seq_blk(KV_WIDTH), seq_blk(KV_WIDTH)] + pages + pages,
        out_specs=seq_blk(ATT_WIDTH),
        scratch_shapes=[pltpu.VMEM((KV_HEADS, GR, HEAD_DIM), F32), pltpu.VMEM((2, KV_HEADS, LANES, HEAD_DIM), F32),
                        pltpu.VMEM((KV_HEADS, GR, bps * CHUNK), F32), part(), part(), part(), part(),
                        pltpu.VMEM((KV_HEADS // 2, GR, LANES), F32)])
    return pl.pallas_call(
        functools.partial(_attn_sample_kernel, n_tok, bps), grid_spec=grid_spec,
        out_shape=jax.ShapeDtypeStruct((B, 8, ATT_WIDTH), F32), name="attn_sample",
        compiler_params=pltpu.CompilerParams(dimension_semantics=("arbitrary", "arbitrary"), vmem_limit_bytes=VMEM_LIMIT),
    )(page_table, q8, kn8, vn8, *([cache_kT] * npg), *([cache_vT] * npg))


def _gated_proj_res_kernel(r_ref, a_ref, g_ref, w_ref, o_ref):
    og = (a_ref[...] * g_ref[...]).astype(BF16)
    o_ref[...] = r_ref[...] + jnp.dot(og, w_ref[...], preferred_element_type=F32)


def _gated_proj_res(res, a, g, w):
    return pl.pallas_call(
        _gated_proj_res_kernel, grid=(1,),
        in_specs=[_const_spec(res.shape), _const_spec(a.shape), _const_spec(g.shape), _const_spec(w.shape)],
        out_specs=_const_spec(res.shape), out_shape=jax.ShapeDtypeStruct(res.shape, F32), name="gated_proj_res",
        compiler_params=pltpu.CompilerParams(vmem_limit_bytes=VMEM_LIMIT),
    )(res, a, g, w)


def _attention_layer(yp, ys, cache_k, cache_v, page_table, norm_g, w_in, qn_g, kn_g, w_out):
    S = yp.shape[1]
    B, T, _ = ys.shape
    ng = norm_g.reshape(1, -1)
    wo = w_out.astype(BF16)
    qT, gT, k_p, v_p, kb, vT, km = _l1_proj(yp.reshape(S, D_MODEL), ng, w_in.T.astype(BF16),
                                            qn_g.reshape(-1, 1), kn_g.reshape(-1, 1))
    yp2 = _attn_prompt(qT, gT, yp.reshape(S, D_MODEL), kb, vT, km.reshape(-1, KV_WIDTH), wo)

    ys2 = ys.reshape(B * T, D_MODEL)
    pair = lambda g: jnp.concatenate([g, g]).reshape(1, LANES)
    q_s, k_s, v_s, g_s = _l1_proj_nat(ys2, ng, w_in.astype(BF16), pair(qn_g), pair(kn_g))
    pad_tok = lambda a: jnp.pad(a.reshape(B, T, -1), ((0, 0), (0, 8 - T), (0, 0)))
    paged = lambda c: jnp.transpose(c, (0, 2, 3, 1))
    o_s = _attn_sample(page_table, pad_tok(q_s), pad_tok(k_s), pad_tok(v_s), paged(cache_k), paged(cache_v), T)
    ys3 = _gated_proj_res(ys2, o_s[:, :T].reshape(B * T, ATT_WIDTH), g_s, wo)
    tok_major = lambda aT: jnp.transpose(aT.reshape(KV_HEADS, HEAD_DIM, S), (2, 0, 1))[None]
    return (yp2.reshape(1, S, D_MODEL), ys3.reshape(B, T, D_MODEL), tok_major(k_p), tok_major(v_p),
            k_s.reshape(B, T, KV_HEADS, HEAD_DIM), v_s.reshape(B, T, KV_HEADS, HEAD_DIM))


def kernel(x_prompt, x_sample, state_ssm, state_ssd_conv, state_conf_conv, cache_k, cache_v, page_table, norm0_g, w_in0, ssd_conv_w, ssd_conv_b, ssd_dt_bias, ssd_a_log, ssd_d, ssd_norm_g, conf_conv_w, conf_conv_b, conf_ln_g, conf_ln_b, w_out0, norm1_g, w_in1, q_norm_g, k_norm_g, w_out1):
    p0 = _prep_l0_params(norm0_g[0], w_in0[0], ssd_conv_w[0], ssd_conv_b[0], ssd_dt_bias[0], ssd_a_log[0], ssd_d[0],
                         ssd_norm_g[0], conf_conv_w[0], conf_conv_b[0], conf_ln_g[0], conf_ln_b[0], w_out0[0])
    yp, ys, h_p, h_s, sh_p, sh_s, ch_p, ch_s = _hybrid_layer(
        x_prompt, x_sample, state_ssm[0], state_ssd_conv[0], state_conf_conv[0], p0)
    yp, ys, k_p, v_p, k_s, v_s = _attention_layer(yp, ys, cache_k[0], cache_v[0], page_table, norm1_g[0], w_in1[0],
                                                  q_norm_g[0], k_norm_g[0], w_out1[0])
    return (yp, ys, h_p[None], h_s[None], sh_p[None], sh_s[None], ch_p[None], ch_s[None],
            k_p[None], v_p[None], k_s[None], v_s[None])
```
